```python
import math
import jax, jax.numpy as jnp
from jax import lax
import numpy as np

D_MODEL = 1024
BATCH = 2
SEQ = 8192
DEPTH = 4

N_META = 16
BLOCK = 128
FRONT = BLOCK
BR_WIDTH = D_MODEL // 2
N_BRANCH = 4
HG_HEADS = 4
HG_DK = BR_WIDTH // HG_HEADS
SB_HEADS = 8
SB_DH = BR_WIDTH // SB_HEADS
POOL_WINDOWS = (2, 4, 8, 16)
POOL_GROUP = BR_WIDTH // len(POOL_WINDOWS)
CONV_WIDTH = 3
D_FF = 4 * D_MODEL
N_SPLIT = 11
IN_COLS = N_SPLIT * BR_WIDTH + N_BRANCH * D_MODEL
EPS = 1e-6

kernel_name = "hybrid_hgrn2_pool_stickbreak_shortconv"

F32 = jnp.float32


def rmsnorm(x, g):
    xf = x.astype(F32)
    y = xf * lax.rsqrt(jnp.mean(xf * xf, axis=-1, keepdims=True) + EPS)
    return (y * g.astype(F32)).astype(x.dtype)


def hgrn2_mixer(q, f_logit, i, g, lb, valid, norm_g):
    B_, L, _ = q.shape
    n_chunks = L // BLOCK
    vmask = valid[None, :, None]
    xf = f_logit.astype(F32)
    log_f = jnp.where(vmask, jnp.log(lb + (1.0 - lb) * jax.nn.sigmoid(xf)), 0.0)
    k = jnp.where(vmask, (1.0 - lb) * jax.nn.sigmoid(-xf), 0.0)
    qf = q.astype(F32)
    vf = jnp.where(vmask, i.astype(F32), 0.0)

    def to_chunks(a):
        return a.reshape(B_, n_chunks, BLOCK, HG_HEADS, HG_DK).transpose(1, 0, 2, 3, 4)

    causal = jnp.tril(jnp.ones((BLOCK, BLOCK), bool))[None, :, :, None, None]

    def step(S, inp):
        qc, kc, ic, lfc = inp
        b = jnp.cumsum(lfc, axis=1)
        o_inter = jnp.einsum('bthk,bhkv->bthv', qc * jnp.exp(b), S)
        diff = b[:, :, None] - b[:, None, :]
        decay = jnp.exp(jnp.where(causal, diff, -jnp.inf))
        scores = jnp.einsum('bthk,btshk,bshk->bhts', qc, decay, kc)
        o_intra = jnp.einsum('bhts,bshv->bthv', scores, ic)
        b_last = b[:, -1]
        S_new = jnp.exp(b_last)[..., None] * S + jnp.einsum(
            'bshk,bshv->bhkv', kc * jnp.exp(b_last[:, None] - b), ic)
        return S_new, o_inter + o_intra

    S0 = jnp.zeros((B_, HG_HEADS, HG_DK, HG_DK), F32)
    _, o = lax.scan(step, S0, (to_chunks(qf), to_chunks(k), to_chunks(vf), to_chunks(log_f)))
    o = o.transpose(1, 0, 2, 3, 4).reshape(B_, L, HG_HEADS, HG_DK)
    o = o * lax.rsqrt(jnp.mean(o * o, axis=-1, keepdims=True) + EPS)
    o = o * norm_g.astype(F32).reshape(HG_HEADS, HG_DK)
    o = o.reshape(B_, L, BR_WIDTH) * jax.nn.sigmoid(g.astype(F32))
    return o.astype(q.dtype)


def multiscale_pool(v, valid, pool_w, pool_scale):
    B_, L, _ = v.shape
    vf = jnp.where(valid[None, :, None], v.astype(F32), 0.0)
    cs = jnp.cumsum(vf, axis=1)
    cnt = jnp.cumsum(valid.astype(F32))
    means = []
    for gi, w in enumerate(POOL_WINDOWS):
        c = cs[:, :, gi * POOL_GROUP:(gi + 1) * POOL_GROUP]
        c_prev = jnp.pad(c, ((0, 0), (w, 0), (0, 0)))[:, :L]
        n_win = jnp.maximum(cnt - jnp.pad(cnt, (w, 0))[:L], 1.0)
        means.append((c - c_prev) / n_win[None, :, None])
    u = (jnp.concatenate(means, axis=-1) - vf).reshape(B_, L, len(POOL_WINDOWS), POOL_GROUP)
    y = jnp.einsum('blgc,gcd->blgd', u, pool_w.astype(F32)).reshape(B_, L, BR_WIDTH)
    return (y * pool_scale.astype(F32)).astype(v.dtype)


def stick_breaking_attention(q, k, v, valid):
    B_, L, _ = q.shape
    n_blk = L // BLOCK
    scale = 1.0 / math.sqrt(SB_DH)
    qh = q.astype(F32).reshape(B_, n_blk, BLOCK, SB_HEADS, SB_DH).transpose(1, 0, 2, 3, 4)
    kh = k.astype(F32).reshape(B_, L, SB_HEADS, SB_DH)
    vh = v.astype(F32).reshape(B_, L, SB_HEADS, SB_DH)
    key_pos = jnp.arange(L)

    def one_block(args):
        q_blk, start = args
        q_pos = start + jnp.arange(BLOCK)
        mask = (key_pos[None, :] < q_pos[:, None]) & valid[None, :]
        z = jnp.einsum('bqhd,bkhd->bhqk', q_blk, kh) * scale
        log_keep = jnp.where(mask, jax.nn.log_sigmoid(-z), 0.0)
        between = lax.cumsum(log_keep, axis=3, reverse=True) - log_keep
        A = jnp.where(mask, jnp.exp(jax.nn.log_sigmoid(z) + between), 0.0)
        return jnp.einsum('bhqk,bkhd->bqhd', A, vh)

    o = lax.map(one_block, (qh, jnp.arange(n_blk) * BLOCK))
    return o.transpose(1, 0, 2, 3, 4).reshape(B_, L, BR_WIDTH).astype(q.dtype)


def short_conv_mixer(h_in, b_gate, c_gate, conv_w, valid):
    u = jnp.where(valid[None, :, None], c_gate * h_in, 0.0).astype(h_in.dtype)
    y = lax.conv_general_dilated(
        u, conv_w.astype(h_in.dtype)[:, None, :], window_strides=(1,),
        padding=[(CONV_WIDTH - 1, 0)], dimension_numbers=('NWC', 'WIO', 'NWC'),
        feature_group_count=BR_WIDTH)
    return b_gate * y


def hybrid_layer(x, valid, lb, norm1_g, w_in, hg_norm_g, pool_w, pool_scale, conv_w,
                 w_branch, w_o, norm2_g, w_up, w_down):
    B_, L, _ = x.shape
    h = rmsnorm(x, norm1_g)
    proj = jnp.einsum('bld,dc->blc', h, w_in)
    split_at = [BR_WIDTH * (j + 1) for j in range(N_SPLIT)]
    (hg_q, hg_f, hg_i, hg_g, pool_v, sb_q, sb_k, sb_v,
     sc_h, sc_b, sc_c, gate_logits) = jnp.split(proj, split_at, axis=-1)

    branches = (
        hgrn2_mixer(hg_q, hg_f, hg_i, hg_g, lb, valid, hg_norm_g),
        multiscale_pool(pool_v, valid, pool_w, pool_scale),
        stick_breaking_attention(sb_q, sb_k, sb_v, valid),
        short_conv_mixer(sc_h, sc_b, sc_c, conv_w, valid),
    )
    gates = jax.nn.sigmoid(gate_logits.astype(F32)).reshape(B_, L, N_BRANCH, D_MODEL)
    mixed = jnp.zeros((B_, L, D_MODEL), F32)
    for n in range(N_BRANCH):
        mixed = mixed + gates[:, :, n] * jnp.einsum('blw,wd->bld', branches[n], w_branch[n]).astype(F32)
    x = x + jnp.einsum('bld,de->ble', mixed.astype(x.dtype), w_o)

    h2 = rmsnorm(x, norm2_g)
    act = jnp.square(jax.nn.relu(jnp.einsum('bld,df->blf', h2, w_up)))
    return x + jnp.einsum('blf,fd->bld', act, w_down)


def setup_inputs(seed: int = 0) -> dict:
    key = jax.random.key(seed)
    ks = jax.random.split(key, 16)

    def nrm(k, shape, fan_in):
        return jax.random.normal(k, shape, F32) * (fan_in ** -0.5)

    def gain(k, shape, s=0.02):
        return 1.0 + s * jax.random.normal(k, shape, F32)

    return {
        "x": jax.random.normal(ks[0], (BATCH, SEQ, D_MODEL), F32),
        "meta_tokens": jax.random.normal(ks[1], (N_META, D_MODEL), F32),
        "lb_logits": 0.5 * jax.random.normal(ks[2], (DEPTH, BR_WIDTH), F32),
        "norm1_g": gain(ks[3], (DEPTH, D_MODEL)),
        "w_in": nrm(ks[4], (DEPTH, D_MODEL, IN_COLS), D_MODEL),
        "hg_norm_g": gain(ks[5], (DEPTH, BR_WIDTH)),
        "pool_w": nrm(ks[6], (DEPTH, len(POOL_WINDOWS), POOL_GROUP, POOL_GROUP), POOL_GROUP),
        "pool_scale": gain(ks[7], (DEPTH, BR_WIDTH), 0.1),
        "conv_w": nrm(ks[8], (DEPTH, CONV_WIDTH, BR_WIDTH), CONV_WIDTH),
        "w_branch": nrm(ks[9], (DEPTH, N_BRANCH, BR_WIDTH, D_MODEL), BR_WIDTH),
        "w_o": nrm(ks[10], (DEPTH, D_MODEL, D_MODEL), D_MODEL),
        "norm2_g": gain(ks[11], (DEPTH, D_MODEL)),
        "w_up": nrm(ks[12], (DEPTH, D_MODEL, D_FF), D_MODEL),
        "w_down": nrm(ks[13], (DEPTH, D_FF, D_MODEL), D_FF),
        "final_norm_g": gain(ks[14], (D_MODEL,)),
    }


def reference(x, meta_tokens, lb_logits, norm1_g, w_in, hg_norm_g, pool_w, pool_scale, conv_w,
              w_branch, w_o, norm2_g, w_up, w_down, final_norm_g):
    B_, S_, _ = x.shape
    dt = x.dtype
    lead = jnp.concatenate([jnp.zeros((FRONT - N_META, D_MODEL), dt), meta_tokens.astype(dt)], axis=0)
    h = jnp.concatenate([jnp.broadcast_to(lead[None], (B_, FRONT, D_MODEL)), x], axis=1)
    L = FRONT + S_
    valid = jnp.arange(L) >= (FRONT - N_META)
    cum = jnp.cumsum(jax.nn.softmax(lb_logits.astype(F32), axis=0), axis=0)
    lower_bounds = cum - cum[0]
    for layer in range(DEPTH):
        h = hybrid_layer(h, valid, lower_bounds[layer], norm1_g[layer], w_in[layer], hg_norm_g[layer],
                         pool_w[layer], pool_scale[layer], conv_w[layer], w_branch[layer], w_o[layer],
                         norm2_g[layer], w_up[layer], w_down[layer])
    return rmsnorm(h[:, FRONT:], final_norm_g)
```

```python
import functools
import math

import jax
import jax.numpy as jnp
import numpy as np
from jax import lax
from jax.experimental import pallas as pl
from jax.experimental.pallas import tpu as pltpu

F32 = jnp.float32
BF16 = jnp.bfloat16

N_META = 16
BLOCK = 128
FRONT = BLOCK
N_MASKED = FRONT - N_META
N_BRANCH = 4
HG_HEADS = 4
SB_HEADS = 8
POOL_WINDOWS = (2, 4, 8, 16)
CONV_WIDTH = 3
N_SPLIT = 11
EPS = 1e-6

SUB = 8
ATT_BLK = 256
VMEM_LIMIT = 52 * 1024 * 1024

_GATE_BLOCKS = 2 * N_BRANCH
(_C_HG_Q, _C_HG_F, _C_HG_I, _C_HG_G, _C_POOL_V, _C_SB_Q, _C_SB_K, _C_SB_V,
 _C_SC_H, _C_SC_B, _C_SC_C) = range(_GATE_BLOCKS, _GATE_BLOCKS + N_SPLIT)


def _pick_tile(n, candidates):
    for c in candidates:
        if n % c == 0:
            return c
    raise ValueError(f"no tile in {candidates} divides {n}")


def _params(*sem):
    return pltpu.CompilerParams(dimension_semantics=sem, vmem_limit_bytes=VMEM_LIMIT)


def _rms(x, g):
    return x * lax.rsqrt(jnp.mean(x * x, axis=-1, keepdims=True) + EPS) * g


def _dot(a, b):
    return jnp.dot(a, b, preferred_element_type=F32)


def _dot_nt(a, b):
    return lax.dot_general(a, b, (((1,), (1,)), ((), ())), preferred_element_type=F32)


def _dot_split(m_bf16, x):
    hi = x.astype(BF16)
    lo = (x - hi.astype(F32)).astype(BF16)
    return _dot(m_bf16, hi) + _dot(m_bf16, lo)


def _inproj_kernel(x_ref, g_ref, w_ref, o_ref, h_ref):
    @pl.when(pl.program_id(1) == 0)
    def _():
        h_ref[...] = _rms(x_ref[...], g_ref[...]).astype(BF16)

    o_ref[...] = _dot(h_ref[...], w_ref[...])


def _inproj(x2d, g, w):
    n, d = x2d.shape
    cols = w.shape[1]
    tm = _pick_tile(n, (768, 512, 256, 128))
    tn = _pick_tile(cols, (2432, 1216, 512, 128))
    return pl.pallas_call(
        _inproj_kernel,
        grid=(n // tm, cols // tn),
        in_specs=[pl.BlockSpec((tm, d), lambda i, j: (i, 0)),
                  pl.BlockSpec((1, d), lambda i, j: (0, 0)),
                  pl.BlockSpec((d, tn), lambda i, j: (0, j))],
        out_specs=pl.BlockSpec((tm, tn), lambda i, j: (i, j)),
        out_shape=jax.ShapeDtypeStruct((n, cols), F32),
        scratch_shapes=[pltpu.VMEM((tm, d), BF16)],
        compiler_params=_params("parallel", "arbitrary"),
        name="inproj",
    )(x2d, g, w)


def _hgrn2_kernel(q_ref, f_ref, i_ref, g_ref, lb_ref, ng_ref, o_ref, st_ref):
    c = pl.program_id(1)
    dk = BLOCK
    nsub = BLOCK // SUB

    @pl.when(c == 0)
    def _():
        st_ref[...] = jnp.zeros_like(st_ref)

    row = lax.broadcasted_iota(jnp.int32, (BLOCK, 1), 0)
    valid = (row + c * BLOCK) >= N_MASKED
    lb = lb_ref[...]
    xf = f_ref[...]
    log_f = jnp.where(valid, jnp.log(lb + (1.0 - lb) * jax.nn.sigmoid(xf)), 0.0)
    kk = jnp.where(valid, (1.0 - lb) * jax.nn.sigmoid(-xf), 0.0)
    vv = jnp.where(valid, i_ref[...], 0.0)
    qq = q_ref[...]

    r2 = lax.broadcasted_iota(jnp.int32, (BLOCK, BLOCK), 0)
    c2 = lax.broadcasted_iota(jnp.int32, (BLOCK, BLOCK), 1)
    tril = (c2 <= r2).astype(BF16)
    ones = jnp.ones((BLOCK, BLOCK), BF16)
    b_all = _dot_split(tril, log_f)
    same_blk_off = c2 - (r2 // SUB) * SUB
    row_in_blk = r2 % SUB

    for h in range(HG_HEADS):
        sl = slice(h * dk, (h + 1) * dk)
        q, k, v, b = qq[:, sl], kk[:, sl], vv[:, sl], b_all[:, sl]
        st = st_ref[h]
        v16 = v.astype(BF16)

        o = _dot_nt((q * jnp.exp(b)).astype(BF16), st.astype(BF16))

        b3 = b.reshape(nsub, SUB, dk)
        k3 = k.reshape(nsub, SUB, dk)
        e_blk = jnp.broadcast_to(b3[:, SUB - 1:SUB, :], (nsub, SUB, dk)).reshape(BLOCK, dk)
        kt = (k * jnp.exp(e_blk - b)).astype(BF16)

        for j in range(nsub - 1):
            r0 = (j + 1) * SUB
            e_j = b[r0 - 1:r0, :]
            qt = (q[r0:, :] * jnp.exp(b[r0:, :] - e_j)).astype(BF16)
            sc = _dot_nt(qt, kt[r0 - SUB:r0, :])
            oj = _dot(sc.astype(BF16), v16[r0 - SUB:r0, :])
            o = o + jnp.concatenate([jnp.zeros((r0, dk), F32), oj], axis=0)

        p = jnp.zeros((BLOCK, BLOCK), F32)
        for s in range(SUB):
            b_s = jnp.broadcast_to(b3[:, s:s + 1, :], (nsub, SUB, dk)).reshape(BLOCK, dk)
            k_s = jnp.broadcast_to(k3[:, s:s + 1, :], (nsub, SUB, dk)).reshape(BLOCK, dk)
            d = q * k_s * jnp.exp(jnp.minimum(b - b_s, 0.0))
            tot = _dot(d.astype(BF16), ones)
            p = p + jnp.where((same_blk_off == s) & (row_in_blk >= s), tot, 0.0)
        o = o + _dot(p.astype(BF16), v16)

        b_last = b[BLOCK - 1:BLOCK, :]
        kd = (k * jnp.exp(b_last - b)).astype(BF16)
        st_ref[h] = st * jnp.exp(b_last) + _dot(v.T.astype(BF16), kd)

        o = o * lax.rsqrt(jnp.mean(o * o, axis=-1, keepdims=True) + EPS) * ng_ref[:, sl]
        o_ref[:, sl] = o * jax.nn.sigmoid(g_ref[:, sl])


def _hgrn2(proj3, lb, ng):
    bsz, lp, _ = proj3.shape
    w = lb.shape[-1]

    def col(cb):
        return pl.BlockSpec((None, BLOCK, w), lambda b, c: (b, c, cb))

    vec = pl.BlockSpec((1, w), lambda b, c: (0, 0))
    return pl.pallas_call(
        _hgrn2_kernel,
        grid=(bsz, lp // BLOCK),
        in_specs=[col(_C_HG_Q), col(_C_HG_F), col(_C_HG_I), col(_C_HG_G), vec, vec],
        out_specs=pl.BlockSpec((None, BLOCK, w), lambda b, c: (b, c, 0)),
        out_shape=jax.ShapeDtypeStruct((bsz, lp, w), F32),
        scratch_shapes=[pltpu.VMEM((HG_HEADS, BLOCK, BLOCK), F32)],
        compiler_params=_params("parallel", "arbitrary"),
        name="hgrn2",
    )(proj3, proj3, proj3, proj3, lb, ng)


POOL_HALO = 16
CONV_HALO = 8


def _local_kernel(pv_ref, ch_ref, cb_ref, cc_ref, pw_ref, ps_ref, cw_ref, po_ref, co_ref,
                  vbuf, ubuf):
    t = pl.program_id(1)
    tm = pv_ref.shape[0]
    grp = pv_ref.shape[1] // len(POOL_WINDOWS)

    @pl.when(t == 0)
    def _():
        vbuf[0:POOL_HALO, :] = jnp.zeros((POOL_HALO, vbuf.shape[1]), F32)
        ubuf[0:CONV_HALO, :] = jnp.zeros((CONV_HALO, ubuf.shape[1]), F32)

    pos = lax.broadcasted_iota(jnp.int32, (tm, 1), 0) + t * tm
    valid = pos >= N_MASKED
    vf = jnp.where(valid, pv_ref[...], 0.0)
    vbuf[POOL_HALO:POOL_HALO + tm, :] = vf
    cnt = jnp.maximum(pos - (N_MASKED - 1), 0).astype(F32)

    for gi, w in enumerate(POOL_WINDOWS):
        sl = slice(gi * grp, (gi + 1) * grp)
        acc = vf[:, sl]
        for d in range(1, w):
            acc = acc + vbuf[POOL_HALO - d:POOL_HALO - d + tm, sl]
        cnt_prev = jnp.maximum(pos - w - (N_MASKED - 1), 0).astype(F32)
        n_win = jnp.maximum(cnt - cnt_prev, 1.0)
        u = acc / n_win - vf[:, sl]
        y = _dot(u.astype(BF16), pw_ref[gi])
        po_ref[:, sl] = y * ps_ref[:, sl]

    u = jnp.where(valid, cc_ref[...] * ch_ref[...], 0.0)
    ubuf[CONV_HALO:CONV_HALO + tm, :] = u
    y = u * cw_ref[CONV_WIDTH - 1:CONV_WIDTH, :]
    for d in range(1, CONV_WIDTH):
        y = y + ubuf[CONV_HALO - d:CONV_HALO - d + tm, :] * cw_ref[CONV_WIDTH - 1 - d:CONV_WIDTH - d, :]
    co_ref[...] = cb_ref[...] * y

    vbuf[0:POOL_HALO, :] = vbuf[tm:tm + POOL_HALO, :]
    ubuf[0:CONV_HALO, :] = ubuf[tm:tm + CONV_HALO, :]


def _local(proj3, pool_w, pool_scale, conv_w):
    bsz, lp, _ = proj3.shape
    w = pool_scale.shape[-1]
    tm = _pick_tile(lp, (768, 512, 256))

    def col(cb):
        return pl.BlockSpec((None, tm, w), lambda b, t: (b, t, cb))

    out = pl.BlockSpec((None, tm, w), lambda b, t: (b, t, 0))
    shp = jax.ShapeDtypeStruct((bsz, lp, w), F32)
    return pl.pallas_call(
        _local_kernel,
        grid=(bsz, lp // tm),
        in_specs=[col(_C_POOL_V), col(_C_SC_H), col(_C_SC_B), col(_C_SC_C),
                  pl.BlockSpec(pool_w.shape, lambda b, t: (0, 0, 0)),
                  pl.BlockSpec((1, w), lambda b, t: (0, 0)),
                  pl.BlockSpec(conv_w.shape, lambda b, t: (0, 0))],
        out_specs=[out, out],
        out_shape=[shp, shp],
        scratch_shapes=[pltpu.VMEM((tm + POOL_HALO, w), F32), pltpu.VMEM((tm + CONV_HALO, w), F32)],
        compiler_params=_params("parallel", "arbitrary"),
        name="local",
    )(proj3, proj3, proj3, proj3, pool_w, pool_scale, conv_w)


def _attn_kernel(qi_ref, kj_ref, q_ref, k_ref, v_ref, o_ref, acc_ref, carry_ref):
    s = pl.program_id(1)
    qi = qi_ref[s]
    kj = kj_ref[s]
    blk = ATT_BLK
    width = q_ref.shape[1]
    n_pair = width // 128
    dh = width // SB_HEADS

    @pl.when(kj == qi)
    def _():
        acc_ref[...] = jnp.zeros_like(acc_ref)
        carry_ref[...] = jnp.zeros_like(carry_ref)

    q = (q_ref[...] * (1.0 / math.sqrt(dh))).astype(BF16)
    k = k_ref[...].astype(BF16)
    v = v_ref[...]

    key_pos = lax.broadcasted_iota(jnp.int32, (blk, 1), 0) + kj * blk
    key_pos = jnp.where(key_pos >= N_MASKED, key_pos, jnp.int32(2 ** 30))
    q_pos = lax.broadcasted_iota(jnp.int32, (1, blk), 1) + qi * blk
    mask = key_pos < q_pos
    rr = lax.broadcasted_iota(jnp.int32, (blk, blk), 0)
    cc = lax.broadcasted_iota(jnp.int32, (blk, blk), 1)
    later = (cc > rr).astype(BF16)
    lane = lax.broadcasted_iota(jnp.int32, (1, 128), 1)

    for p in range(n_pair):
        sl = slice(p * 128, (p + 1) * 128)
        kp = k[:, sl]
        qp = q[:, sl]
        vt = v[:, sl].T.astype(BF16)
        halves = []
        for hh in range(2):
            h = 2 * p + hh
            qm = jnp.where((lane >= dh) if hh else (lane < dh), qp, jnp.zeros_like(qp))
            z = _dot_nt(kp, qm)
            softplus = jnp.maximum(z, 0.0) + jnp.log(1.0 + jnp.exp(-jnp.abs(z)))
            log_keep = jnp.where(mask, -softplus, 0.0)
            between = _dot_split(later, log_keep) + carry_ref[h:h + 1, :]
            a = jnp.where(mask, jnp.exp(z + log_keep + between), 0.0)
            carry_ref[h:h + 1, :] += jnp.sum(log_keep, axis=0, keepdims=True)
            ot = _dot(vt, a.astype(BF16))
            halves.append(ot[hh * dh:(hh + 1) * dh, :])
        acc_ref[p] += jnp.concatenate(halves, axis=0)

    @pl.when(kj == 0)
    def _():
        for p in range(n_pair):
            o_ref[:, p * 128:(p + 1) * 128] = acc_ref[p].T


def _attn(proj3, w):
    bsz, lp, _ = proj3.shape
    nblk = lp // ATT_BLK
    qi_tbl = np.array([i for i in range(nblk) for _ in range(i + 1)], np.int32)
    kj_tbl = np.array([j for i in range(nblk) for j in range(i, -1, -1)], np.int32)

    grid_spec = pltpu.PrefetchScalarGridSpec(
        num_scalar_prefetch=2,
        grid=(bsz, len(qi_tbl)),
        in_specs=[pl.BlockSpec((None, ATT_BLK, w), lambda b, s, qi, kj: (b, qi[s], _C_SB_Q)),
                  pl.BlockSpec((None, ATT_BLK, w), lambda b, s, qi, kj: (b, kj[s], _C_SB_K)),
                  pl.BlockSpec((None, ATT_BLK, w), lambda b, s, qi, kj: (b, kj[s], _C_SB_V))],
        out_specs=pl.BlockSpec((None, ATT_BLK, w), lambda b, s, qi, kj: (b, qi[s], 0)),
        scratch_shapes=[pltpu.VMEM((w // 128, 128, ATT_BLK), F32),
                        pltpu.VMEM((SB_HEADS, ATT_BLK), F32)],
    )
    return pl.pallas_call(
        _attn_kernel,
        grid_spec=grid_spec,
        out_shape=jax.ShapeDtypeStruct((bsz, lp, w), F32),
        compiler_params=_params("parallel", "arbitrary"),
        name="attn",
    )(jnp.asarray(qi_tbl), jnp.asarray(kj_tbl), proj3, proj3, proj3)


def _merge_kernel(x_ref, g_ref, b0_ref, b1_ref, b2_ref, b3_ref, wb_ref, wo_ref, o_ref):
    d = x_ref.shape[1]
    mixed = jnp.zeros(x_ref.shape, F32)
    for n, b_ref in enumerate((b0_ref, b1_ref, b2_ref, b3_ref)):
        y = _dot(b_ref[...].astype(BF16), wb_ref[n])
        mixed = mixed + jax.nn.sigmoid(g_ref[:, n * d:(n + 1) * d]) * y
    o_ref[...] = x_ref[...] + _dot(mixed.astype(BF16), wo_ref[...])


def _merge(x2d, proj2d, branches, w_branch, w_o):
    n, d = x2d.shape
    w = branches[0].shape[-1]
    tm = _pick_tile(n, (512, 256, 128))
    row = lambda i: (i, 0)
    const1 = pl.Buffered(1)
    return pl.pallas_call(
        _merge_kernel,
        grid=(n // tm,),
        in_specs=[pl.BlockSpec((tm, d), row),
                  pl.BlockSpec((tm, N_BRANCH * d), row)] +
                 [pl.BlockSpec((tm, w), row)] * N_BRANCH +
                 [pl.BlockSpec(w_branch.shape, lambda i: (0, 0, 0), pipeline_mode=const1),
                  pl.BlockSpec(w_o.shape, lambda i: (0, 0), pipeline_mode=const1)],
        out_specs=pl.BlockSpec((tm, d), row),
        out_shape=jax.ShapeDtypeStruct((n, d), F32),
        compiler_params=_params("parallel"),
        name="merge",
    )(x2d, proj2d, *branches, w_branch, w_o)


def _mlp_kernel(x_ref, g_ref, wu_ref, wd_ref, fg_ref, o_ref, *, final_norm, ff_chunk):
    x = x_ref[...]
    h = _rms(x, g_ref[...]).astype(BF16)
    acc = x
    for c in range(wu_ref.shape[1] // ff_chunk):
        a = jnp.maximum(_dot(h, wu_ref[:, c * ff_chunk:(c + 1) * ff_chunk]), 0.0)
        acc = acc + _dot((a * a).astype(BF16), wd_ref[c * ff_chunk:(c + 1) * ff_chunk, :])
    if final_norm:
        acc = _rms(acc, fg_ref[...])
    o_ref[...] = acc


def _mlp(x2d, g, w_up, w_down, final_g, final_norm):
    n, d = x2d.shape
    tm = _pick_tile(n, (512, 256, 128))
    row = lambda i: (i, 0)
    vec = pl.BlockSpec((1, d), lambda i: (0, 0))
    const1 = pl.Buffered(1)
    return pl.pallas_call(
        functools.partial(_mlp_kernel, final_norm=final_norm, ff_chunk=1024),
        grid=(n // tm,),
        in_specs=[pl.BlockSpec((tm, d), row), vec,
                  pl.BlockSpec(w_up.shape, lambda i: (0, 0), pipeline_mode=const1),
                  pl.BlockSpec(w_down.shape, lambda i: (0, 0), pipeline_mode=const1),
                  vec],
        out_specs=pl.BlockSpec((tm, d), row),
        out_shape=jax.ShapeDtypeStruct((n, d), F32),
        compiler_params=_params("parallel"),
        name="mlp",
    )(x2d, g, w_up, w_down, final_g)


def kernel(x, meta_tokens, lb_logits, norm1_g, w_in, hg_norm_g, pool_w, pool_scale, conv_w,
           w_branch, w_o, norm2_g, w_up, w_down, final_norm_g):
    bsz, seq, d = x.shape
    depth = w_in.shape[0]
    w = d // 2
    assert w % 128 == 0 and w // HG_HEADS == BLOCK and w // len(POOL_WINDOWS) == 128
    assert w_in.shape[2] == N_SPLIT * w + N_BRANCH * d

    length = FRONT + seq
    lp = -(-length // ATT_BLK) * ATT_BLK
    lead = jnp.concatenate([jnp.zeros((N_MASKED, d), x.dtype), meta_tokens.astype(x.dtype)], axis=0)
    h = jnp.concatenate([jnp.broadcast_to(lead[None], (bsz, FRONT, d)), x,
                         jnp.zeros((bsz, lp - length, d), x.dtype)], axis=1)
    h = h.reshape(bsz * lp, d)

    cum = jnp.cumsum(jax.nn.softmax(lb_logits.astype(F32), axis=0), axis=0)
    lower_bounds = cum - cum[0]

    w_in_r = jnp.concatenate([w_in[:, :, N_SPLIT * w:], w_in[:, :, :N_SPLIT * w]], axis=2).astype(BF16)
    w_branch16, w_o16 = w_branch.astype(BF16), w_o.astype(BF16)
    w_up16, w_down16 = w_up.astype(BF16), w_down.astype(BF16)
    pool_w16 = pool_w.astype(BF16)
    fg = final_norm_g.reshape(1, d)

    for layer in range(depth):
        proj = _inproj(h, norm1_g[layer].reshape(1, d), w_in_r[layer])
        proj3 = proj.reshape(bsz, lp, -1)
        b_hg = _hgrn2(proj3, lower_bounds[layer].reshape(1, w), hg_norm_g[layer].reshape(1, w))
        b_pool, b_conv = _local(proj3, pool_w16[layer], pool_scale[layer].reshape(1, w), conv_w[layer])
        b_sb = _attn(proj3, w)
        branches = [b.reshape(bsz * lp, w) for b in (b_hg, b_pool, b_sb, b_conv)]
        h = _merge(h, proj, branches, w_branch16[layer], w_o16[layer])
        h = _mlp(h, norm2_g[layer].reshape(1, d), w_up16[layer], w_down16[layer], fg,
                 final_norm=(layer == depth - 1))

    return h.reshape(bsz, lp, d)[:, FRONT:length]
```

```python
import functools
import math

import jax
import jax.numpy as jnp
from jax import lax
from jax.experimental import pallas as pl
from jax.experimental.pallas import tpu as pltpu

F32 = jnp.float32
BF16 = jnp.bfloat16

N_META = 16
BLOCK = 128
FRONT = BLOCK
N_MASKED = FRONT - N_META
N_BRANCH = 4
HG_HEADS = 4
SB_HEADS = 8
POOL_WINDOWS = (2, 4, 8, 16)
CONV_WIDTH = 3
N_SPLIT = 11
EPS = 1e-6

LANES = 128
SUB = 8
ATT_BLK = 256
ATT_LOOKAHEAD = 2
MASKED_SCORE = -1e30
VMEM_LIMIT = 52 * 1024 * 1024

_GATE_BLOCKS = 2 * N_BRANCH
(_C_HG_Q, _C_HG_F, _C_HG_I, _C_HG_G, _C_POOL_V, _C_SC_H, _C_SC_B, _C_SC_C) = range(
    _GATE_BLOCKS, _GATE_BLOCKS + 8)


def _pick_tile(n, candidates):
    for c in candidates:
        if n % c == 0:
            return c
    raise ValueError(f"no tile in {candidates} divides {n}")


def _params(*sem):
    return pltpu.CompilerParams(dimension_semantics=sem, vmem_limit_bytes=VMEM_LIMIT)


def _rms(x, g):
    return x * lax.rsqrt(jnp.mean(x * x, axis=-1, keepdims=True) + EPS) * g


def _dot(a, b):
    return jnp.dot(a, b, preferred_element_type=F32)


def _dot_nt(a, b):
    return lax.dot_general(a, b, (((1,), (1,)), ((), ())), preferred_element_type=F32)


def _dot_split(m_bf16, x):
    hi = x.astype(BF16)
    lo = (x - hi.astype(F32)).astype(BF16)
    return _dot(m_bf16, hi) + _dot(m_bf16, lo)


def _inproj_kernel(x_ref, g_ref, w_ref, o_ref, h_ref):
    @pl.when(pl.program_id(1) == 0)
    def _():
        h_ref[...] = _rms(x_ref[...], g_ref[...]).astype(BF16)

    o_ref[...] = _dot(h_ref[...], w_ref[...])


def _inproj(x2d, g, w):
    n, d = x2d.shape
    cols = w.shape[1]
    tm = _pick_tile(n, (768, 512, 256, 128))
    tn = _pick_tile(cols, (2048, 1024, 512))
    return pl.pallas_call(
        _inproj_kernel,
        grid=(n // tm, cols // tn),
        in_specs=[pl.BlockSpec((tm, d), lambda i, j: (i, 0)),
                  pl.BlockSpec((1, d), lambda i, j: (0, 0)),
                  pl.BlockSpec((d, tn), lambda i, j: (0, j))],
        out_specs=pl.BlockSpec((tm, tn), lambda i, j: (i, j)),
        out_shape=jax.ShapeDtypeStruct((n, cols), F32),
        scratch_shapes=[pltpu.VMEM((tm, d), BF16)],
        compiler_params=_params("parallel", "arbitrary"),
        name="inproj",
    )(x2d, g, w)


def _qkvproj_kernel(x_ref, g_ref, w_ref, qt_ref, k_ref, vt_ref, *, q_scale):
    w = k_ref.shape[1]
    h = _rms(x_ref[...], g_ref[...]).astype(BF16)
    y = _dot(h, w_ref[...])
    qt_ref[...] = (y[:, 0:w] * q_scale).T.astype(BF16)
    k_ref[...] = y[:, w:2 * w].astype(BF16)
    pos = lax.broadcasted_iota(jnp.int32, (x_ref.shape[0], 1), 0) + pl.program_id(1) * x_ref.shape[0]
    vt_ref[...] = jnp.where(pos >= N_MASKED, y[:, 2 * w:3 * w], 0.0).T.astype(BF16)


def _qkvproj(x3d, g, w_qkv):
    bsz, lp, d = x3d.shape
    w = w_qkv.shape[1] // 3
    tm = _pick_tile(lp, (768, 512, 256))
    q_scale = math.log2(math.e) / math.sqrt(w // SB_HEADS)
    row = pl.BlockSpec((None, tm, w), lambda b, t: (b, t, 0))
    colm = pl.BlockSpec((None, w, tm), lambda b, t: (b, 0, t))
    return pl.pallas_call(
        functools.partial(_qkvproj_kernel, q_scale=q_scale),
        grid=(bsz, lp // tm),
        in_specs=[pl.BlockSpec((None, tm, d), lambda b, t: (b, t, 0)),
                  pl.BlockSpec((1, d), lambda b, t: (0, 0)),
                  pl.BlockSpec(w_qkv.shape, lambda b, t: (0, 0))],
        out_specs=[colm, row, colm],
        out_shape=[jax.ShapeDtypeStruct((bsz, w, lp), BF16),
                   jax.ShapeDtypeStruct((bsz, lp, w), BF16),
                   jax.ShapeDtypeStruct((bsz, w, lp), BF16)],
        compiler_params=_params("parallel", "parallel"),
        name="qkvproj",
    )(x3d, g, w_qkv)


def _hgrn2_kernel(q_ref, f_ref, i_ref, g_ref, lb_ref, ng_ref, o_ref, st_ref):
    c = pl.program_id(1)
    dk = BLOCK
    nsub = BLOCK // SUB

    @pl.when(c == 0)
    def _():
        st_ref[...] = jnp.zeros_like(st_ref)

    row = lax.broadcasted_iota(jnp.int32, (BLOCK, 1), 0)
    valid = (row + c * BLOCK) >= N_MASKED
    lb = lb_ref[...]
    xf = f_ref[...]
    log_f = jnp.where(valid, jnp.log(lb + (1.0 - lb) * jax.nn.sigmoid(xf)), 0.0)
    kk = jnp.where(valid, (1.0 - lb) * jax.nn.sigmoid(-xf), 0.0)
    vv = jnp.where(valid, i_ref[...], 0.0)
    qq = q_ref[...]

    r2 = lax.broadcasted_iota(jnp.int32, (BLOCK, BLOCK), 0)
    c2 = lax.broadcasted_iota(jnp.int32, (BLOCK, BLOCK), 1)
    tril = (c2 <= r2).astype(BF16)
    ones = jnp.ones((BLOCK, BLOCK), BF16)
    b_all = _dot_split(tril, log_f)
    same_blk_off = c2 - (r2 // SUB) * SUB
    row_in_blk = r2 % SUB

    for h in range(HG_HEADS):
        sl = slice(h * dk, (h + 1) * dk)
        q, k, v, b = qq[:, sl], kk[:, sl], vv[:, sl], b_all[:, sl]
        st = st_ref[h]
        v16 = v.astype(BF16)

        o = _dot_nt((q * jnp.exp(b)).astype(BF16), st.astype(BF16))

        b3 = b.reshape(nsub, SUB, dk)
        k3 = k.reshape(nsub, SUB, dk)
        e_blk = jnp.broadcast_to(b3[:, SUB - 1:SUB, :], (nsub, SUB, dk)).reshape(BLOCK, dk)
        kt = (k * jnp.exp(e_blk - b)).astype(BF16)

        for j in range(nsub - 1):
            r0 = (j + 1) * SUB
            e_j = b[r0 - 1:r0, :]
            qt = (q[r0:, :] * jnp.exp(b[r0:, :] - e_j)).astype(BF16)
            sc = _dot_nt(qt, kt[r0 - SUB:r0, :])
            oj = _dot(sc.astype(BF16), v16[r0 - SUB:r0, :])
            o = o + jnp.concatenate([jnp.zeros((r0, dk), F32), oj], axis=0)

        p = jnp.zeros((BLOCK, BLOCK), F32)
        for s in range(SUB):
            b_s = jnp.broadcast_to(b3[:, s:s + 1, :], (nsub, SUB, dk)).reshape(BLOCK, dk)
            k_s = jnp.broadcast_to(k3[:, s:s + 1, :], (nsub, SUB, dk)).reshape(BLOCK, dk)
            d = q * k_s * jnp.exp(jnp.minimum(b - b_s, 0.0))
            tot = _dot(d.astype(BF16), ones)
            p = p + jnp.where((same_blk_off == s) & (row_in_blk >= s), tot, 0.0)
        o = o + _dot(p.astype(BF16), v16)

        b_last = b[BLOCK - 1:BLOCK, :]
        kd = (k * jnp.exp(b_last - b)).astype(BF16)
        st_ref[h] = st * jnp.exp(b_last) + _dot(v.T.astype(BF16), kd)

        o = o * lax.rsqrt(jnp.mean(o * o, axis=-1, keepdims=True) + EPS) * ng_ref[:, sl]
        o_ref[:, sl] = (o * jax.nn.sigmoid(g_ref[:, sl])).astype(o_ref.dtype)


def _hgrn2(proj3, lb, ng):
    bsz, lp, _ = proj3.shape
    w = lb.shape[-1]

    def col(cb):
        return pl.BlockSpec((None, BLOCK, w), lambda b, c: (b, c, cb))

    vec = pl.BlockSpec((1, w), lambda b, c: (0, 0))
    return pl.pallas_call(
        _hgrn2_kernel,
        grid=(bsz, lp // BLOCK),
        in_specs=[col(_C_HG_Q), col(_C_HG_F), col(_C_HG_I), col(_C_HG_G), vec, vec],
        out_specs=pl.BlockSpec((None, BLOCK, w), lambda b, c: (b, c, 0)),
        out_shape=jax.ShapeDtypeStruct((bsz, lp, w), BF16),
        scratch_shapes=[pltpu.VMEM((HG_HEADS, BLOCK, BLOCK), F32)],
        compiler_params=_params("parallel", "arbitrary"),
        name="hgrn2",
    )(proj3, proj3, proj3, proj3, lb, ng)


POOL_HALO = 16
CONV_HALO = 8


def _local_kernel(pv_ref, ch_ref, cb_ref, cc_ref, pw_ref, ps_ref, cw_ref, po_ref, co_ref,
                  vbuf, ubuf):
    t = pl.program_id(1)
    tm = pv_ref.shape[0]
    grp = pv_ref.shape[1] // len(POOL_WINDOWS)

    @pl.when(t == 0)
    def _():
        vbuf[0:POOL_HALO, :] = jnp.zeros((POOL_HALO, vbuf.shape[1]), F32)
        ubuf[0:CONV_HALO, :] = jnp.zeros((CONV_HALO, ubuf.shape[1]), F32)

    pos = lax.broadcasted_iota(jnp.int32, (tm, 1), 0) + t * tm
    valid = pos >= N_MASKED
    vf = jnp.where(valid, pv_ref[...], 0.0)
    vbuf[POOL_HALO:POOL_HALO + tm, :] = vf
    cnt = jnp.maximum(pos - (N_MASKED - 1), 0).astype(F32)

    for gi, w in enumerate(POOL_WINDOWS):
        sl = slice(gi * grp, (gi + 1) * grp)
        acc = vf[:, sl]
        for d in range(1, w):
            acc = acc + vbuf[POOL_HALO - d:POOL_HALO - d + tm, sl]
        cnt_prev = jnp.maximum(pos - w - (N_MASKED - 1), 0).astype(F32)
        n_win = jnp.maximum(cnt - cnt_prev, 1.0)
        u = acc / n_win - vf[:, sl]
        y = _dot(u.astype(BF16), pw_ref[gi])
        po_ref[:, sl] = (y * ps_ref[:, sl]).astype(po_ref.dtype)

    u = jnp.where(valid, cc_ref[...] * ch_ref[...], 0.0)
    ubuf[CONV_HALO:CONV_HALO + tm, :] = u
    y = u * cw_ref[CONV_WIDTH - 1:CONV_WIDTH, :]
    for d in range(1, CONV_WIDTH):
        y = y + ubuf[CONV_HALO - d:CONV_HALO - d + tm, :] * cw_ref[CONV_WIDTH - 1 - d:CONV_WIDTH - d, :]
    co_ref[...] = (cb_ref[...] * y).astype(co_ref.dtype)

    vbuf[0:POOL_HALO, :] = vbuf[tm:tm + POOL_HALO, :]
    ubuf[0:CONV_HALO, :] = ubuf[tm:tm + CONV_HALO, :]


def _local(proj3, pool_w, pool_scale, conv_w):
    bsz, lp, _ = proj3.shape
    w = pool_scale.shape[-1]
    tm = _pick_tile(lp, (768, 512, 256))

    def col(cb):
        return pl.BlockSpec((None, tm, w), lambda b, t: (b, t, cb))

    out = pl.BlockSpec((None, tm, w), lambda b, t: (b, t, 0))
    shp = jax.ShapeDtypeStruct((bsz, lp, w), BF16)
    return pl.pallas_call(
        _local_kernel,
        grid=(bsz, lp // tm),
        in_specs=[col(_C_POOL_V), col(_C_SC_H), col(_C_SC_B), col(_C_SC_C),
                  pl.BlockSpec(pool_w.shape, lambda b, t: (0, 0, 0)),
                  pl.BlockSpec((1, w), lambda b, t: (0, 0)),
                  pl.BlockSpec(conv_w.shape, lambda b, t: (0, 0))],
        out_specs=[out, out],
        out_shape=[shp, shp],
        scratch_shapes=[pltpu.VMEM((tm + POOL_HALO, w), F32), pltpu.VMEM((tm + CONV_HALO, w), F32)],
        compiler_params=_params("parallel", "arbitrary"),
        name="local",
    )(proj3, proj3, proj3, proj3, pool_w, pool_scale, conv_w)


def _attn_kernel(qt_ref, k_ref, vt_ref, o_ref, qm_ref, acc_ref, carry_ref, lb_ref, sp_ref):
    qi = pl.program_id(1)
    blk = ATT_BLK
    width = qt_ref.shape[0]
    n_pair = width // LANES
    dh = width // SB_HEADS

    acc_ref[...] = jnp.zeros_like(acc_ref)
    carry_ref[...] = jnp.zeros_like(carry_ref)
    zeros = jnp.zeros((dh, blk), BF16)
    for p in range(n_pair):
        qm_ref[2 * p] = jnp.concatenate([qt_ref[p * LANES:p * LANES + dh, :], zeros], axis=0)
        qm_ref[2 * p + 1] = jnp.concatenate([zeros, qt_ref[p * LANES + dh:(p + 1) * LANES, :]], axis=0)

    rr = lax.broadcasted_iota(jnp.int32, (blk, blk), 0)
    cc = lax.broadcasted_iota(jnp.int32, (blk, blk), 1)
    later = (cc > rr).astype(BF16)
    sign_bit = jnp.uint32(0x80000000)

    def step(score_blk, score_par, out_blk, out_par, diagonal=False):
        later_sums, scores = {}, {}

        def issue_later_sums(h):
            later_sums[h] = _dot(later, sp_ref[out_par, h]) + jnp.tile(carry_ref[h], (blk // 8, 1))

        def issue_scores(h):
            ks = pl.multiple_of(score_blk * blk, blk)
            p = h // 2
            scores[h] = _dot(k_ref[pl.ds(ks, blk), p * LANES:(p + 1) * LANES], qm_ref[h])

        def finish_scores(h):
            z = scores[h]
            if diagonal:
                key_loc = lax.broadcasted_iota(jnp.int32, (blk, 1), 0)
                q_loc = lax.broadcasted_iota(jnp.int32, (1, blk), 1)
                z = jnp.where(key_loc < q_loc, z, MASKED_SCORE)
            neg_abs = lax.bitcast_convert_type(
                lax.bitcast_convert_type(z, jnp.uint32) | sign_bit, F32)
            sp = jnp.maximum(z, 0.0) + jnp.log2(1.0 + jnp.exp2(neg_abs))
            lb_ref[score_par, h] = z - sp
            sp_ref[score_par, h] = sp.astype(BF16)

        def finish_output(h):
            ks = pl.multiple_of(out_blk * blk, blk)
            p, hh = h // 2, h % 2
            later_sum = later_sums[h]
            a = jnp.exp2(lb_ref[out_par, h] - later_sum).astype(BF16)
            first_sp = sp_ref[out_par, h, 0:1, :].astype(F32)
            carry_ref[h] = jnp.broadcast_to(later_sum[0:1, :] + first_sp, carry_ref.shape[1:])
            vt = vt_ref[p * LANES:(p + 1) * LANES, pl.ds(ks, blk)]
            ot = _dot(vt, a)
            acc_ref[p, hh * dh:(hh + 1) * dh, :] += ot[hh * dh:(hh + 1) * dh, :]

        for t in range(SB_HEADS + ATT_LOOKAHEAD):
            if t < SB_HEADS:
                if out_blk is not None:
                    issue_later_sums(t)
                if score_blk is not None:
                    issue_scores(t)
            h = t - ATT_LOOKAHEAD
            if h >= 0:
                if out_blk is not None:
                    finish_output(h)
                if score_blk is not None:
                    finish_scores(h)

    step(qi, 0, None, None, diagonal=True)

    def body(i, c):
        for par in range(2):
            @pl.when((i & 1) == par)
            def _():
                step(qi - i, par, qi - i + 1, 1 - par)
        return c

    lax.fori_loop(1, qi + 1, body, 0)

    for par in range(2):
        @pl.when((qi & 1) == par)
        def _():
            step(None, None, 0, par)

    for p in range(n_pair):
        o_ref[:, p * LANES:(p + 1) * LANES] = acc_ref[p].T.astype(o_ref.dtype)


def _attn(qt, k, vt):
    bsz, w, lp = qt.shape
    return pl.pallas_call(
        _attn_kernel,
        grid=(bsz, lp // ATT_BLK),
        in_specs=[pl.BlockSpec((None, w, ATT_BLK), lambda b, i: (b, 0, i)),
                  pl.BlockSpec((None, lp, w), lambda b, i: (b, 0, 0)),
                  pl.BlockSpec((None, w, lp), lambda b, i: (b, 0, 0))],
        out_specs=pl.BlockSpec((None, ATT_BLK, w), lambda b, i: (b, i, 0)),
        out_shape=jax.ShapeDtypeStruct((bsz, lp, w), BF16),
        scratch_shapes=[pltpu.VMEM((SB_HEADS, LANES, ATT_BLK), BF16),
                        pltpu.VMEM((w // LANES, LANES, ATT_BLK), F32),
                        pltpu.VMEM((SB_HEADS, 8, ATT_BLK), F32),
                        pltpu.VMEM((2, SB_HEADS, ATT_BLK, ATT_BLK), F32),
                        pltpu.VMEM((2, SB_HEADS, ATT_BLK, ATT_BLK), BF16)],
        compiler_params=_params("parallel", "arbitrary"),
        name="attn",
    )(qt, k, vt)


def _merge_kernel(x_ref, g_ref, b0_ref, b1_ref, b2_ref, b3_ref, wb_ref, wo_ref, o_ref):
    d = x_ref.shape[1]
    mixed = jnp.zeros(x_ref.shape, F32)
    for n, b_ref in enumerate((b0_ref, b1_ref, b2_ref, b3_ref)):
        y = _dot(b_ref[...], wb_ref[n])
        mixed = mixed + jax.nn.sigmoid(g_ref[:, n * d:(n + 1) * d]) * y
    o_ref[...] = x_ref[...] + _dot(mixed.astype(BF16), wo_ref[...])


def _merge(x2d, proj2d, branches, w_branch, w_o):
    n, d = x2d.shape
    w = branches[0].shape[-1]
    tm = _pick_tile(n, (512, 256, 128))
    row = lambda i: (i, 0)
    const1 = pl.Buffered(1)
    return pl.pallas_call(
        _merge_kernel,
        grid=(n // tm,),
        in_specs=[pl.BlockSpec((tm, d), row),
                  pl.BlockSpec((tm, N_BRANCH * d), row)] +
                 [pl.BlockSpec((tm, w), row)] * N_BRANCH +
                 [pl.BlockSpec(w_branch.shape, lambda i: (0, 0, 0), pipeline_mode=const1),
                  pl.BlockSpec(w_o.shape, lambda i: (0, 0), pipeline_mode=const1)],
        out_specs=pl.BlockSpec((tm, d), row),
        out_shape=jax.ShapeDtypeStruct((n, d), F32),
        compiler_params=_params("parallel"),
        name="merge",
    )(x2d, proj2d, *branches, w_branch, w_o)


def _mlp_kernel(x_ref, g_ref, wu_ref, wd_ref, fg_ref, o_ref, *, final_norm, ff_chunk):
    x = x_ref[...]
    h = _rms(x, g_ref[...]).astype(BF16)
    acc = x
    for c in range(wu_ref.shape[1] // ff_chunk):
        a = jnp.maximum(_dot(h, wu_ref[:, c * ff_chunk:(c + 1) * ff_chunk]), 0.0)
        acc = acc + _dot((a * a).astype(BF16), wd_ref[c * ff_chunk:(c + 1) * ff_chunk, :])
    if final_norm:
        acc = _rms(acc, fg_ref[...])
    o_ref[...] = acc


def _mlp(x2d, g, w_up, w_down, final_g, final_norm):
    n, d = x2d.shape
    tm = _pick_tile(n, (512, 256, 128))
    row = lambda i: (i, 0)
    vec = pl.BlockSpec((1, d), lambda i: (0, 0))
    const1 = pl.Buffered(1)
    return pl.pallas_call(
        functools.partial(_mlp_kernel, final_norm=final_norm, ff_chunk=1024),
        grid=(n // tm,),
        in_specs=[pl.BlockSpec((tm, d), row), vec,
                  pl.BlockSpec(w_up.shape, lambda i: (0, 0), pipeline_mode=const1),
                  pl.BlockSpec(w_down.shape, lambda i: (0, 0), pipeline_mode=const1),
                  vec],
        out_specs=pl.BlockSpec((tm, d), row),
        out_shape=jax.ShapeDtypeStruct((n, d), F32),
        compiler_params=_params("parallel"),
        name="mlp",
    )(x2d, g, w_up, w_down, final_g)


def kernel(x, meta_tokens, lb_logits, norm1_g, w_in, hg_norm_g, pool_w, pool_scale, conv_w,
           w_branch, w_o, norm2_g, w_up, w_down, final_norm_g):
    bsz, seq, d = x.shape
    depth = w_in.shape[0]
    w = d // 2
    assert w % LANES == 0 and w // HG_HEADS == BLOCK and w // len(POOL_WINDOWS) == LANES
    assert w_in.shape[2] == N_SPLIT * w + N_BRANCH * d

    length = FRONT + seq
    lp = -(-length // ATT_BLK) * ATT_BLK
    lead = jnp.concatenate([jnp.zeros((N_MASKED, d), x.dtype), meta_tokens.astype(x.dtype)], axis=0)
    h = jnp.concatenate([jnp.broadcast_to(lead[None], (bsz, FRONT, d)), x,
                         jnp.zeros((bsz, lp - length, d), x.dtype)], axis=1)
    h = h.reshape(bsz * lp, d)

    cum = jnp.cumsum(jax.nn.softmax(lb_logits.astype(F32), axis=0), axis=0)
    lower_bounds = cum - cum[0]

    w_main = jnp.concatenate([w_in[:, :, N_SPLIT * w:], w_in[:, :, :5 * w], w_in[:, :, 8 * w:N_SPLIT * w]],
                             axis=2).astype(BF16)
    w_qkv = w_in[:, :, 5 * w:8 * w].astype(BF16)
    w_branch16, w_o16 = w_branch.astype(BF16), w_o.astype(BF16)
    w_up16, w_down16 = w_up.astype(BF16), w_down.astype(BF16)
    pool_w16 = pool_w.astype(BF16)
    fg = final_norm_g.reshape(1, d)

    for layer in range(depth):
        g1 = norm1_g[layer].reshape(1, d)
        proj = _inproj(h, g1, w_main[layer])
        qt, k, vt = _qkvproj(h.reshape(bsz, lp, d), g1, w_qkv[layer])
        proj3 = proj.reshape(bsz, lp, -1)
        b_hg = _hgrn2(proj3, lower_bounds[layer].reshape(1, w), hg_norm_g[layer].reshape(1, w))
        b_pool, b_conv = _local(proj3, pool_w16[layer], pool_scale[layer].reshape(1, w), conv_w[layer])
        b_sb = _attn(qt, k, vt)
        branches = [b.reshape(bsz * lp, w) for b in (b_hg, b_pool, b_sb, b_conv)]
        h = _merge(h, proj, branches, w_branch16[layer], w_o16[layer])
        h = _mlp(h, norm2_g[layer].reshape(1, d), w_up16[layer], w_down16[layer], fg,
                 final_norm=(layer == depth - 1))

    return h.reshape(bsz, lp, d)[:, FRONT:length]
```

```python
import functools
import math

import jax
import jax.numpy as jnp
from jax import lax
from jax.experimental import pallas as pl
from jax.experimental.pallas import tpu as pltpu

F32 = jnp.float32
BF16 = jnp.bfloat16

N_META = 16
BLOCK = 128
FRONT = BLOCK
N_MASKED = FRONT - N_META
N_BRANCH = 4
HG_HEADS = 4
SB_HEADS = 8
POOL_WINDOWS = (2, 4, 8, 16)
CONV_WIDTH = 3
N_SPLIT = 11
EPS = 1e-6

LANES = 128
SUB = 8
ATT_BLK = 256
ATT_LOOKAHEAD = 2
ZERO_WEIGHT_CARRY = 160.0
MASKED_SCORE = -1e30
VMEM_LIMIT = 52 * 1024 * 1024

_GATE_BLOCKS = 2 * N_BRANCH
(_C_HG_Q, _C_HG_F, _C_HG_I, _C_HG_G, _C_POOL_V, _C_SC_H, _C_SC_B, _C_SC_C) = range(
    _GATE_BLOCKS, _GATE_BLOCKS + 8)


def _pick_tile(n, candidates):
    for c in candidates:
        if n % c == 0:
            return c
    raise ValueError(f"no tile in {candidates} divides {n}")


def _params(*sem):
    return pltpu.CompilerParams(dimension_semantics=sem, vmem_limit_bytes=VMEM_LIMIT)


def _rms(x, g):
    return x * lax.rsqrt(jnp.mean(x * x, axis=-1, keepdims=True) + EPS) * g


def _dot(a, b):
    return jnp.dot(a, b, preferred_element_type=F32)


def _dot_nt(a, b):
    return lax.dot_general(a, b, (((1,), (1,)), ((), ())), preferred_element_type=F32)


def _dot_split(m_bf16, x):
    hi = x.astype(BF16)
    lo = (x - hi.astype(F32)).astype(BF16)
    return _dot(m_bf16, hi) + _dot(m_bf16, lo)


def _inproj_kernel(x_ref, g_ref, w_ref, o_ref, h_ref):
    @pl.when(pl.program_id(1) == 0)
    def _():
        h_ref[...] = _rms(x_ref[...], g_ref[...]).astype(BF16)

    o_ref[...] = _dot(h_ref[...], w_ref[...])


def _inproj(x2d, g, w):
    n, d = x2d.shape
    cols = w.shape[1]
    tm = _pick_tile(n, (768, 512, 256, 128))
    tn = _pick_tile(cols, (2048, 1024, 512))
    return pl.pallas_call(
        _inproj_kernel,
        grid=(n // tm, cols // tn),
        in_specs=[pl.BlockSpec((tm, d), lambda i, j: (i, 0)),
                  pl.BlockSpec((1, d), lambda i, j: (0, 0)),
                  pl.BlockSpec((d, tn), lambda i, j: (0, j))],
        out_specs=pl.BlockSpec((tm, tn), lambda i, j: (i, j)),
        out_shape=jax.ShapeDtypeStruct((n, cols), F32),
        scratch_shapes=[pltpu.VMEM((tm, d), BF16)],
        compiler_params=_params("parallel", "arbitrary"),
        name="inproj",
    )(x2d, g, w)


def _qkvproj_kernel(x_ref, g_ref, w_ref, qt_ref, k_ref, vt_ref, *, q_scale):
    w = k_ref.shape[1]
    h = _rms(x_ref[...], g_ref[...]).astype(BF16)
    y = _dot(h, w_ref[...])
    qt_ref[...] = (y[:, 0:w] * q_scale).T.astype(BF16)
    k_ref[...] = y[:, w:2 * w].astype(BF16)
    pos = lax.broadcasted_iota(jnp.int32, (x_ref.shape[0], 1), 0) + pl.program_id(1) * x_ref.shape[0]
    vt_ref[...] = jnp.where(pos >= N_MASKED, y[:, 2 * w:3 * w], 0.0).T.astype(BF16)


def _qkvproj(x3d, g, w_qkv):
    bsz, lp, d = x3d.shape
    w = w_qkv.shape[1] // 3
    tm = _pick_tile(lp, (768, 512, 256))
    q_scale = math.log2(math.e) / math.sqrt(w // SB_HEADS)
    row = pl.BlockSpec((None, tm, w), lambda b, t: (b, t, 0))
    colm = pl.BlockSpec((None, w, tm), lambda b, t: (b, 0, t))
    return pl.pallas_call(
        functools.partial(_qkvproj_kernel, q_scale=q_scale),
        grid=(bsz, lp // tm),
        in_specs=[pl.BlockSpec((None, tm, d), lambda b, t: (b, t, 0)),
                  pl.BlockSpec((1, d), lambda b, t: (0, 0)),
                  pl.BlockSpec(w_qkv.shape, lambda b, t: (0, 0))],
        out_specs=[colm, row, colm],
        out_shape=[jax.ShapeDtypeStruct((bsz, w, lp), BF16),
                   jax.ShapeDtypeStruct((bsz, lp, w), BF16),
                   jax.ShapeDtypeStruct((bsz, w, lp), BF16)],
        compiler_params=_params("parallel", "parallel"),
        name="qkvproj",
    )(x3d, g, w_qkv)


def _hgrn2_kernel(q_ref, f_ref, i_ref, g_ref, lb_ref, ng_ref, o_ref, st_ref):
    c = pl.program_id(1)
    dk = BLOCK
    nsub = BLOCK // SUB

    @pl.when(c == 0)
    def _():
        st_ref[...] = jnp.zeros_like(st_ref)

    row = lax.broadcasted_iota(jnp.int32, (BLOCK, 1), 0)
    valid = (row + c * BLOCK) >= N_MASKED
    lb = lb_ref[...]
    xf = f_ref[...]
    log_f = jnp.where(valid, jnp.log(lb + (1.0 - lb) * jax.nn.sigmoid(xf)), 0.0)
    kk = jnp.where(valid, (1.0 - lb) * jax.nn.sigmoid(-xf), 0.0)
    vv = jnp.where(valid, i_ref[...], 0.0)
    qq = q_ref[...]

    r2 = lax.broadcasted_iota(jnp.int32, (BLOCK, BLOCK), 0)
    c2 = lax.broadcasted_iota(jnp.int32, (BLOCK, BLOCK), 1)
    tril = (c2 <= r2).astype(BF16)
    ones = jnp.ones((BLOCK, BLOCK), BF16)
    b_all = _dot_split(tril, log_f)
    same_blk_off = c2 - (r2 // SUB) * SUB
    row_in_blk = r2 % SUB

    for h in range(HG_HEADS):
        sl = slice(h * dk, (h + 1) * dk)
        q, k, v, b = qq[:, sl], kk[:, sl], vv[:, sl], b_all[:, sl]
        st = st_ref[h]
        v16 = v.astype(BF16)

        o = _dot_nt((q * jnp.exp(b)).astype(BF16), st.astype(BF16))

        b3 = b.reshape(nsub, SUB, dk)
        k3 = k.reshape(nsub, SUB, dk)
        e_blk = jnp.broadcast_to(b3[:, SUB - 1:SUB, :], (nsub, SUB, dk)).reshape(BLOCK, dk)
        kt = (k * jnp.exp(e_blk - b)).astype(BF16)

        for j in range(nsub - 1):
            r0 = (j + 1) * SUB
            e_j = b[r0 - 1:r0, :]
            qt = (q[r0:, :] * jnp.exp(b[r0:, :] - e_j)).astype(BF16)
            sc = _dot_nt(qt, kt[r0 - SUB:r0, :])
            oj = _dot(sc.astype(BF16), v16[r0 - SUB:r0, :])
            o = o + jnp.concatenate([jnp.zeros((r0, dk), F32), oj], axis=0)

        p = jnp.zeros((BLOCK, BLOCK), F32)
        for s in range(SUB):
            b_s = jnp.broadcast_to(b3[:, s:s + 1, :], (nsub, SUB, dk)).reshape(BLOCK, dk)
            k_s = jnp.broadcast_to(k3[:, s:s + 1, :], (nsub, SUB, dk)).reshape(BLOCK, dk)
            d = q * k_s * jnp.exp(jnp.minimum(b - b_s, 0.0))
            tot = _dot(d.astype(BF16), ones)
            p = p + jnp.where((same_blk_off == s) & (row_in_blk >= s), tot, 0.0)
        o = o + _dot(p.astype(BF16), v16)

        b_last = b[BLOCK - 1:BLOCK, :]
        kd = (k * jnp.exp(b_last - b)).astype(BF16)
        st_ref[h] = st * jnp.exp(b_last) + _dot(v.T.astype(BF16), kd)

        o = o * lax.rsqrt(jnp.mean(o * o, axis=-1, keepdims=True) + EPS) * ng_ref[:, sl]
        o_ref[:, sl] = (o * jax.nn.sigmoid(g_ref[:, sl])).astype(o_ref.dtype)


def _hgrn2(proj3, lb, ng):
    bsz, lp, _ = proj3.shape
    w = lb.shape[-1]

    def col(cb):
        return pl.BlockSpec((None, BLOCK, w), lambda b, c: (b, c, cb))

    vec = pl.BlockSpec((1, w), lambda b, c: (0, 0))
    return pl.pallas_call(
        _hgrn2_kernel,
        grid=(bsz, lp // BLOCK),
        in_specs=[col(_C_HG_Q), col(_C_HG_F), col(_C_HG_I), col(_C_HG_G), vec, vec],
        out_specs=pl.BlockSpec((None, BLOCK, w), lambda b, c: (b, c, 0)),
        out_shape=jax.ShapeDtypeStruct((bsz, lp, w), BF16),
        scratch_shapes=[pltpu.VMEM((HG_HEADS, BLOCK, BLOCK), F32)],
        compiler_params=_params("parallel", "arbitrary"),
        name="hgrn2",
    )(proj3, proj3, proj3, proj3, lb, ng)


POOL_HALO = 16
CONV_HALO = 8


def _local_kernel(pv_ref, ch_ref, cb_ref, cc_ref, pw_ref, ps_ref, cw_ref, po_ref, co_ref,
                  vbuf, ubuf):
    t = pl.program_id(1)
    tm = pv_ref.shape[0]
    grp = pv_ref.shape[1] // len(POOL_WINDOWS)

    @pl.when(t == 0)
    def _():
        vbuf[0:POOL_HALO, :] = jnp.zeros((POOL_HALO, vbuf.shape[1]), F32)
        ubuf[0:CONV_HALO, :] = jnp.zeros((CONV_HALO, ubuf.shape[1]), F32)

    pos = lax.broadcasted_iota(jnp.int32, (tm, 1), 0) + t * tm
    valid = pos >= N_MASKED
    vf = jnp.where(valid, pv_ref[...], 0.0)
    vbuf[POOL_HALO:POOL_HALO + tm, :] = vf
    cnt = jnp.maximum(pos - (N_MASKED - 1), 0).astype(F32)

    for gi, w in enumerate(POOL_WINDOWS):
        sl = slice(gi * grp, (gi + 1) * grp)
        acc = vf[:, sl]
        for d in range(1, w):
            acc = acc + vbuf[POOL_HALO - d:POOL_HALO - d + tm, sl]
        cnt_prev = jnp.maximum(pos - w - (N_MASKED - 1), 0).astype(F32)
        n_win = jnp.maximum(cnt - cnt_prev, 1.0)
        u = acc / n_win - vf[:, sl]
        y = _dot(u.astype(BF16), pw_ref[gi])
        po_ref[:, sl] = (y * ps_ref[:, sl]).astype(po_ref.dtype)

    u = jnp.where(valid, cc_ref[...] * ch_ref[...], 0.0)
    ubuf[CONV_HALO:CONV_HALO + tm, :] = u
    y = u * cw_ref[CONV_WIDTH - 1:CONV_WIDTH, :]
    for d in range(1, CONV_WIDTH):
        y = y + ubuf[CONV_HALO - d:CONV_HALO - d + tm, :] * cw_ref[CONV_WIDTH - 1 - d:CONV_WIDTH - d, :]
    co_ref[...] = (cb_ref[...] * y).astype(co_ref.dtype)

    vbuf[0:POOL_HALO, :] = vbuf[tm:tm + POOL_HALO, :]
    ubuf[0:CONV_HALO, :] = ubuf[tm:tm + CONV_HALO, :]


def _local(proj3, pool_w, pool_scale, conv_w):
    bsz, lp, _ = proj3.shape
    w = pool_scale.shape[-1]
    tm = _pick_tile(lp, (768, 512, 256))

    def col(cb):
        return pl.BlockSpec((None, tm, w), lambda b, t: (b, t, cb))

    out = pl.BlockSpec((None, tm, w), lambda b, t: (b, t, 0))
    shp = jax.ShapeDtypeStruct((bsz, lp, w), BF16)
    return pl.pallas_call(
        _local_kernel,
        grid=(bsz, lp // tm),
        in_specs=[col(_C_POOL_V), col(_C_SC_H), col(_C_SC_B), col(_C_SC_C),
                  pl.BlockSpec(pool_w.shape, lambda b, t: (0, 0, 0)),
                  pl.BlockSpec((1, w), lambda b, t: (0, 0)),
                  pl.BlockSpec(conv_w.shape, lambda b, t: (0, 0))],
        out_specs=[out, out],
        out_shape=[shp, shp],
        scratch_shapes=[pltpu.VMEM((tm + POOL_HALO, w), F32), pltpu.VMEM((tm + CONV_HALO, w), F32)],
        compiler_params=_params("parallel", "arbitrary"),
        name="local",
    )(proj3, proj3, proj3, proj3, pool_w, pool_scale, conv_w)


def _attn_kernel(qt_ref, k_ref, vt_ref, o_ref, qm_ref, acc_ref, carry_ref, lb_ref, sp_ref):
    qi = pl.program_id(1)
    blk = ATT_BLK
    width = qt_ref.shape[0]
    n_pair = width // LANES
    dh = width // SB_HEADS

    acc_ref[...] = jnp.zeros_like(acc_ref)
    carry_ref[...] = jnp.zeros_like(carry_ref)
    zeros = jnp.zeros((dh, blk), BF16)
    for p in range(n_pair):
        qm_ref[2 * p] = jnp.concatenate([qt_ref[p * LANES:p * LANES + dh, :], zeros], axis=0)
        qm_ref[2 * p + 1] = jnp.concatenate([zeros, qt_ref[p * LANES + dh:(p + 1) * LANES, :]], axis=0)

    rr = lax.broadcasted_iota(jnp.int32, (blk, blk), 0)
    cc = lax.broadcasted_iota(jnp.int32, (blk, blk), 1)
    later = (cc > rr).astype(BF16)
    sign_bit = jnp.uint32(0x80000000)

    def step(score_blk, score_par, out_blk, out_par, diagonal=False):
        later_sums, scores = {}, {}

        def issue_later_sums(h):
            later_sums[h] = _dot(later, sp_ref[out_par, h])

        def issue_scores(h):
            ks = pl.multiple_of(score_blk * blk, blk)
            p = h // 2
            scores[h] = _dot(k_ref[pl.ds(ks, blk), p * LANES:(p + 1) * LANES], qm_ref[h])

        def finish_scores(h):
            z = scores[h]
            if diagonal:
                key_loc = lax.broadcasted_iota(jnp.int32, (blk, 1), 0)
                q_loc = lax.broadcasted_iota(jnp.int32, (1, blk), 1)
                z = jnp.where(key_loc < q_loc, z, MASKED_SCORE)
            neg_abs = lax.bitcast_convert_type(
                lax.bitcast_convert_type(z, jnp.uint32) | sign_bit, F32)
            sp = jnp.maximum(z, 0.0) + jnp.log2(1.0 + jnp.exp2(neg_abs))
            lb_ref[score_par, h] = z - sp
            sp_ref[score_par, h] = sp.astype(BF16)

        def finish_output(h):
            ks = pl.multiple_of(out_blk * blk, blk)
            p, hh = h // 2, h % 2
            later_sum = later_sums[h]
            a = jnp.exp2(lb_ref[out_par, h] - later_sum).astype(BF16)
            carry = carry_ref[h]
            first_sp = sp_ref[out_par, h, 0:1, :].astype(F32)
            carry_ref[h] = carry + (later_sum[0:1, :] + first_sp)
            vt = vt_ref[p * LANES:(p + 1) * LANES, pl.ds(ks, blk)]
            ot = _dot(vt, a)
            scale = jnp.tile(jnp.exp2(-carry), (dh // 8, 1))
            acc_ref[p, hh * dh:(hh + 1) * dh, :] += ot[hh * dh:(hh + 1) * dh, :] * scale

        for t in range(SB_HEADS + ATT_LOOKAHEAD):
            if t < SB_HEADS:
                if out_blk is not None:
                    issue_later_sums(t)
                if score_blk is not None:
                    issue_scores(t)
            h = t - ATT_LOOKAHEAD
            if h >= 0:
                if out_blk is not None:
                    finish_output(h)
                if score_blk is not None:
                    finish_scores(h)

    step(qi, 0, None, None, diagonal=True)

    def keep_walking(state):
        i, min_carry = state
        return jnp.logical_and(i <= qi, min_carry < ZERO_WEIGHT_CARRY)

    def walk(state):
        i, _ = state
        for par in range(2):
            @pl.when((i & 1) == par)
            def _():
                step(qi - i, par, qi - i + 1, 1 - par)
        return i + 1, jnp.min(carry_ref[...])

    _, min_carry = lax.while_loop(keep_walking, walk, (jnp.int32(1), jnp.float32(0.0)))

    for par in range(2):
        @pl.when(jnp.logical_and((qi & 1) == par, min_carry < ZERO_WEIGHT_CARRY))
        def _():
            step(None, None, 0, par)

    for p in range(n_pair):
        o_ref[:, p * LANES:(p + 1) * LANES] = acc_ref[p].T.astype(o_ref.dtype)


def _attn(qt, k, vt):
    bsz, w, lp = qt.shape
    return pl.pallas_call(
        _attn_kernel,
        grid=(bsz, lp // ATT_BLK),
        in_specs=[pl.BlockSpec((None, w, ATT_BLK), lambda b, i: (b, 0, i)),
                  pl.BlockSpec((None, lp, w), lambda b, i: (b, 0, 0)),
                  pl.BlockSpec((None, w, lp), lambda b, i: (b, 0, 0))],
        out_specs=pl.BlockSpec((None, ATT_BLK, w), lambda b, i: (b, i, 0)),
        out_shape=jax.ShapeDtypeStruct((bsz, lp, w), BF16),
        scratch_shapes=[pltpu.VMEM((SB_HEADS, LANES, ATT_BLK), BF16),
                        pltpu.VMEM((w // LANES, LANES, ATT_BLK), F32),
                        pltpu.VMEM((SB_HEADS, 8, ATT_BLK), F32),
                        pltpu.VMEM((2, SB_HEADS, ATT_BLK, ATT_BLK), F32),
                        pltpu.VMEM((2, SB_HEADS, ATT_BLK, ATT_BLK), BF16)],
        compiler_params=_params("parallel", "arbitrary"),
        name="attn",
    )(qt, k, vt)


def _merge_kernel(x_ref, g_ref, b0_ref, b1_ref, b2_ref, b3_ref, wb_ref, wo_ref, o_ref):
    d = x_ref.shape[1]
    mixed = jnp.zeros(x_ref.shape, F32)
    for n, b_ref in enumerate((b0_ref, b1_ref, b2_ref, b3_ref)):
        y = _dot(b_ref[...], wb_ref[n])
        mixed = mixed + jax.nn.sigmoid(g_ref[:, n * d:(n + 1) * d]) * y
    o_ref[...] = x_ref[...] + _dot(mixed.astype(BF16), wo_ref[...])


def _merge(x2d, proj2d, branches, w_branch, w_o):
    n, d = x2d.shape
    w = branches[0].shape[-1]
    tm = _pick_tile(n, (512, 256, 128))
    row = lambda i: (i, 0)
    const1 = pl.Buffered(1)
    return pl.pallas_call(
        _merge_kernel,
        grid=(n // tm,),
        in_specs=[pl.BlockSpec((tm, d), row),
                  pl.BlockSpec((tm, N_BRANCH * d), row)] +
                 [pl.BlockSpec((tm, w), row)] * N_BRANCH +
                 [pl.BlockSpec(w_branch.shape, lambda i: (0, 0, 0), pipeline_mode=const1),
                  pl.BlockSpec(w_o.shape, lambda i: (0, 0), pipeline_mode=const1)],
        out_specs=pl.BlockSpec((tm, d), row),
        out_shape=jax.ShapeDtypeStruct((n, d), F32),
        compiler_params=_params("parallel"),
        name="merge",
    )(x2d, proj2d, *branches, w_branch, w_o)


def _mlp_kernel(x_ref, g_ref, wu_ref, wd_ref, fg_ref, o_ref, *, final_norm, ff_chunk):
    x = x_ref[...]
    h = _rms(x, g_ref[...]).astype(BF16)
    acc = x
    for c in range(wu_ref.shape[1] // ff_chunk):
        a = jnp.maximum(_dot(h, wu_ref[:, c * ff_chunk:(c + 1) * ff_chunk]), 0.0)
        acc = acc + _dot((a * a).astype(BF16), wd_ref[c * ff_chunk:(c + 1) * ff_chunk, :])
    if final_norm:
        acc = _rms(acc, fg_ref[...])
    o_ref[...] = acc


def _mlp(x2d, g, w_up, w_down, final_g, final_norm):
    n, d = x2d.shape
    tm = _pick_tile(n, (512, 256, 128))
    row = lambda i: (i, 0)
    vec = pl.BlockSpec((1, d), lambda i: (0, 0))
    const1 = pl.Buffered(1)
    return pl.pallas_call(
        functools.partial(_mlp_kernel, final_norm=final_norm, ff_chunk=1024),
        grid=(n // tm,),
        in_specs=[pl.BlockSpec((tm, d), row), vec,
                  pl.BlockSpec(w_up.shape, lambda i: (0, 0), pipeline_mode=const1),
                  pl.BlockSpec(w_down.shape, lambda i: (0, 0), pipeline_mode=const1),
                  vec],
        out_specs=pl.BlockSpec((tm, d), row),
        out_shape=jax.ShapeDtypeStruct((n, d), F32),
        compiler_params=_params("parallel"),
        name="mlp",
    )(x2d, g, w_up, w_down, final_g)


def kernel(x, meta_tokens, lb_logits, norm1_g, w_in, hg_norm_g, pool_w, pool_scale, conv_w,
           w_branch, w_o, norm2_g, w_up, w_down, final_norm_g):
    bsz, seq, d = x.shape
    depth = w_in.shape[0]
    w = d // 2
    assert w % LANES == 0 and w // HG_HEADS == BLOCK and w // len(POOL_WINDOWS) == LANES
    assert w_in.shape[2] == N_SPLIT * w + N_BRANCH * d

    length = FRONT + seq
    lp = -(-length // ATT_BLK) * ATT_BLK
    lead = jnp.concatenate([jnp.zeros((N_MASKED, d), x.dtype), meta_tokens.astype(x.dtype)], axis=0)
    h = jnp.concatenate([jnp.broadcast_to(lead[None], (bsz, FRONT, d)), x,
                         jnp.zeros((bsz, lp - length, d), x.dtype)], axis=1)
    h = h.reshape(bsz * lp, d)

    cum = jnp.cumsum(jax.nn.softmax(lb_logits.astype(F32), axis=0), axis=0)
    lower_bounds = cum - cum[0]

    w_main = jnp.concatenate([w_in[:, :, N_SPLIT * w:], w_in[:, :, :5 * w], w_in[:, :, 8 * w:N_SPLIT * w]],
                             axis=2).astype(BF16)
    w_qkv = w_in[:, :, 5 * w:8 * w].astype(BF16)
    w_branch16, w_o16 = w_branch.astype(BF16), w_o.astype(BF16)
    w_up16, w_down16 = w_up.astype(BF16), w_down.astype(BF16)
    pool_w16 = pool_w.astype(BF16)
    fg = final_norm_g.reshape(1, d)

    for layer in range(depth):
        g1 = norm1_g[layer].reshape(1, d)
        proj = _inproj(h, g1, w_main[layer])
        qt, k, vt = _qkvproj(h.reshape(bsz, lp, d), g1, w_qkv[layer])
        proj3 = proj.reshape(bsz, lp, -1)
        b_hg = _hgrn2(proj3, lower_bounds[layer].reshape(1, w), hg_norm_g[layer].reshape(1, w))
        b_pool, b_conv = _local(proj3, pool_w16[layer], pool_scale[layer].reshape(1, w), conv_w[layer])
        b_sb = _attn(qt, k, vt)
        branches = [b.reshape(bsz * lp, w) for b in (b_hg, b_pool, b_sb, b_conv)]
        h = _merge(h, proj, branches, w_branch16[layer], w_o16[layer])
        h = _mlp(h, norm2_g[layer].reshape(1, d), w_up16[layer], w_down16[layer], fg,
                 final_norm=(layer == depth - 1))

    return h.reshape(bsz, lp, d)[:, FRONT:length]
```

```python
import functools
import math

import jax
import jax.numpy as jnp
from jax import lax
from jax.experimental import pallas as pl
from jax.experimental.pallas import tpu as pltpu

F32 = jnp.float32
BF16 = jnp.bfloat16

N_META = 16
BLOCK = 128
FRONT = BLOCK
N_MASKED = FRONT - N_META
N_BRANCH = 4
HG_HEADS = 4
SB_HEADS = 8
POOL_WINDOWS = (2, 4, 8, 16)
CONV_WIDTH = 3
N_SPLIT = 11
EPS = 1e-6

LANES = 128
SUB = 8
ATT_BLK = 256
ATT_LOOKAHEAD = 2
ZERO_WEIGHT_CARRY = 160.0
MASKED_SCORE = -1e30
VMEM_LIMIT = 52 * 1024 * 1024

_GATE_BLOCKS = 2 * N_BRANCH
(_C_HG_Q, _C_HG_F, _C_HG_I, _C_HG_G, _C_POOL_V, _C_SC_H, _C_SC_B, _C_SC_C) = range(
    _GATE_BLOCKS, _GATE_BLOCKS + 8)


def _pick_tile(n, candidates):
    for c in candidates:
        if n % c == 0:
            return c
    raise ValueError(f"no tile in {candidates} divides {n}")


def _params(*sem):
    return pltpu.CompilerParams(dimension_semantics=sem, vmem_limit_bytes=VMEM_LIMIT)


def _rms(x, g):
    return x * lax.rsqrt(jnp.mean(x * x, axis=-1, keepdims=True) + EPS) * g


def _dot(a, b):
    return jnp.dot(a, b, preferred_element_type=F32)


def _dot_nt(a, b):
    return lax.dot_general(a, b, (((1,), (1,)), ((), ())), preferred_element_type=F32)


def _dot_split(m_bf16, x):
    hi = x.astype(BF16)
    lo = (x - hi.astype(F32)).astype(BF16)
    return _dot(m_bf16, hi) + _dot(m_bf16, lo)


def _inproj_kernel(x_ref, g_ref, w_ref, o_ref, h_ref):
    @pl.when(pl.program_id(1) == 0)
    def _():
        h_ref[...] = _rms(x_ref[...], g_ref[...]).astype(BF16)

    o_ref[...] = _dot(h_ref[...], w_ref[...]).astype(o_ref.dtype)


def _inproj(x2d, g, w):
    n, d = x2d.shape
    cols = w.shape[1]
    tm = _pick_tile(n, (768, 512, 256, 128))
    tn = _pick_tile(cols, (2048, 1024, 512))
    return pl.pallas_call(
        _inproj_kernel,
        grid=(n // tm, cols // tn),
        in_specs=[pl.BlockSpec((tm, d), lambda i, j: (i, 0)),
                  pl.BlockSpec((1, d), lambda i, j: (0, 0)),
                  pl.BlockSpec((d, tn), lambda i, j: (0, j))],
        out_specs=pl.BlockSpec((tm, tn), lambda i, j: (i, j)),
        out_shape=jax.ShapeDtypeStruct((n, cols), BF16),
        scratch_shapes=[pltpu.VMEM((tm, d), BF16)],
        compiler_params=_params("parallel", "arbitrary"),
        name="inproj",
    )(x2d, g, w)


def _qkvproj_kernel(x_ref, g_ref, w_ref, qt_ref, k_ref, vt_ref, *, q_scale):
    w = k_ref.shape[1]
    h = _rms(x_ref[...], g_ref[...]).astype(BF16)
    y = _dot(h, w_ref[...])
    qt_ref[...] = (y[:, 0:w] * q_scale).T.astype(BF16)
    k_ref[...] = y[:, w:2 * w].astype(BF16)
    pos = lax.broadcasted_iota(jnp.int32, (x_ref.shape[0], 1), 0) + pl.program_id(1) * x_ref.shape[0]
    vt_ref[...] = jnp.where(pos >= N_MASKED, y[:, 2 * w:3 * w], 0.0).T.astype(BF16)


def _qkvproj(x3d, g, w_qkv):
    bsz, lp, d = x3d.shape
    w = w_qkv.shape[1] // 3
    tm = _pick_tile(lp, (768, 512, 256))
    q_scale = math.log2(math.e) / math.sqrt(w // SB_HEADS)
    row = pl.BlockSpec((None, tm, w), lambda b, t: (b, t, 0))
    colm = pl.BlockSpec((None, w, tm), lambda b, t: (b, 0, t))
    return pl.pallas_call(
        functools.partial(_qkvproj_kernel, q_scale=q_scale),
        grid=(bsz, lp // tm),
        in_specs=[pl.BlockSpec((None, tm, d), lambda b, t: (b, t, 0)),
                  pl.BlockSpec((1, d), lambda b, t: (0, 0)),
                  pl.BlockSpec(w_qkv.shape, lambda b, t: (0, 0))],
        out_specs=[colm, row, colm],
        out_shape=[jax.ShapeDtypeStruct((bsz, w, lp), BF16),
                   jax.ShapeDtypeStruct((bsz, lp, w), BF16),
                   jax.ShapeDtypeStruct((bsz, w, lp), BF16)],
        compiler_params=_params("parallel", "parallel"),
        name="qkvproj",
    )(x3d, g, w_qkv)


HG_LEVELS = (64, 32, 16, 8, 4, 2, 1)
LOG2E = math.log2(math.e)


def _hgrn2_kernel(q_ref, f_ref, i_ref, g_ref, lb_ref, ng_ref, o_ref, st_ref):
    step_i = pl.program_id(1)
    tm, width = q_ref.shape
    dk = BLOCK
    tiles = BLOCK // SUB

    @pl.when(step_i == 0)
    def _():
        st_ref[...] = jnp.zeros_like(st_ref)

    r2 = lax.broadcasted_iota(jnp.int32, (BLOCK, BLOCK), 0)
    c2 = lax.broadcasted_iota(jnp.int32, (BLOCK, BLOCK), 1)
    tril = (c2 <= r2).astype(BF16)
    differ = r2 ^ c2
    level = jnp.zeros((BLOCK, BLOCK), jnp.int32)
    for c in reversed(HG_LEVELS):
        level = jnp.where(differ >= c, c, level)
    level = jnp.where(c2 > r2, -1, level)
    sub_row = lax.broadcasted_iota(jnp.int32, (1, SUB, 1), 1)
    sign_bit = jnp.uint32(0x80000000)
    lb = lb_ref[...]

    def chunk(ci, carry):
        r0 = pl.multiple_of(ci * BLOCK, BLOCK)
        rows = pl.ds(r0, BLOCK)
        pos = lax.broadcasted_iota(jnp.int32, (BLOCK, 1), 0) + (step_i * tm + r0)
        valid = pos >= N_MASKED
        xf = f_ref[rows, :].astype(F32)
        log_f = jnp.where(valid, jnp.log(lb + (1.0 - lb) * jax.nn.sigmoid(xf)), 0.0)
        kk = jnp.where(valid, (1.0 - lb) * jax.nn.sigmoid(-xf), 0.0)
        vv = jnp.where(valid, i_ref[rows, :].astype(F32), 0.0)
        qq = q_ref[rows, :].astype(F32)
        b2 = _dot_split(tril, log_f) * LOG2E

        b3 = b2.reshape(tiles, SUB, width)

        def tile_row(r):
            return jnp.broadcast_to(b3[:, r:r + 1, :], b3.shape)

        last3 = tile_row(SUB - 1)
        q_lv, k_lv = {}, {}
        for c in HG_LEVELS:
            if c >= SUB:
                t8 = c // SUB
                idx = [(t // t8) * t8 - 1 if (t // t8) % 2 else (t // t8 + 1) * t8 - 1 for t in range(tiles)]
                ref3 = jnp.concatenate([last3[i:i + 1] for i in idx], axis=0)
            elif c == 4:
                ref3 = tile_row(3)
            elif c == 2:
                ref3 = jnp.where(sub_row < 4, tile_row(1), tile_row(5))
            else:
                ref3 = jnp.where(sub_row % 2 == 1, pltpu.roll(b3, 1, axis=1), b3)
            d = b2 - ref3.reshape(BLOCK, width)
            g = jnp.exp2(lax.bitcast_convert_type(lax.bitcast_convert_type(d, jnp.uint32) | sign_bit, F32))
            q_lv[c] = (qq * g).astype(BF16)
            k_lv[c] = (kk * g).astype(BF16)
        q_lv[0], k_lv[0] = qq.astype(BF16), kk.astype(BF16)

        b_last = b2[BLOCK - 1:BLOCK, :]
        q_in = (qq * jnp.exp2(b2)).astype(BF16)
        k_out = (kk * jnp.exp2(b_last - b2)).astype(BF16)
        keep = jnp.exp2(b_last)
        v16 = vv.astype(BF16)

        for h in range(HG_HEADS):
            sl = slice(h * dk, (h + 1) * dk)
            st = st_ref[h]
            o_inter = _dot_nt(q_in[:, sl], st.astype(BF16))
            scores = {c: _dot_nt(q_lv[c][:, sl], k_lv[c][:, sl]) for c in (0,) + HG_LEVELS}
            p = jnp.zeros((BLOCK, BLOCK), F32)
            for c in (0,) + HG_LEVELS:
                p = jnp.where(level == c, scores[c], p)
            o = o_inter + _dot(p.astype(BF16), v16[:, sl])
            st_ref[h] = st * keep[:, sl] + _dot(vv[:, sl].T.astype(BF16), k_out[:, sl])

            o = o * lax.rsqrt(jnp.mean(o * o, axis=-1, keepdims=True) + EPS) * ng_ref[:, sl]
            o_ref[rows, sl] = (o * jax.nn.sigmoid(g_ref[rows, sl].astype(F32))).astype(o_ref.dtype)
        return carry

    lax.fori_loop(0, tm // BLOCK, chunk, 0)


def _hgrn2(proj3, lb, ng):
    bsz, lp, _ = proj3.shape
    w = lb.shape[-1]
    tm = _pick_tile(lp, (768, 512, 256))

    def col(cb):
        return pl.BlockSpec((None, tm, w), lambda b, c: (b, c, cb))

    vec = pl.BlockSpec((1, w), lambda b, c: (0, 0))
    return pl.pallas_call(
        _hgrn2_kernel,
        grid=(bsz, lp // tm),
        in_specs=[col(_C_HG_Q), col(_C_HG_F), col(_C_HG_I), col(_C_HG_G), vec, vec],
        out_specs=pl.BlockSpec((None, tm, w), lambda b, c: (b, c, 0)),
        out_shape=jax.ShapeDtypeStruct((bsz, lp, w), BF16),
        scratch_shapes=[pltpu.VMEM((HG_HEADS, BLOCK, BLOCK), F32)],
        compiler_params=_params("parallel", "arbitrary"),
        name="hgrn2",
    )(proj3, proj3, proj3, proj3, lb, ng)


POOL_HALO = 16
CONV_HALO = 8


def _local_kernel(pv_ref, ch_ref, cb_ref, cc_ref, pw_ref, ps_ref, cw_ref, po_ref, co_ref,
                  vbuf, ubuf):
    t = pl.program_id(1)
    tm = pv_ref.shape[0]
    grp = pv_ref.shape[1] // len(POOL_WINDOWS)

    @pl.when(t == 0)
    def _():
        vbuf[0:POOL_HALO, :] = jnp.zeros((POOL_HALO, vbuf.shape[1]), F32)
        ubuf[0:CONV_HALO, :] = jnp.zeros((CONV_HALO, ubuf.shape[1]), F32)

    pos = lax.broadcasted_iota(jnp.int32, (tm, 1), 0) + t * tm
    valid = pos >= N_MASKED
    vf = jnp.where(valid, pv_ref[...].astype(F32), 0.0)
    vbuf[POOL_HALO:POOL_HALO + tm, :] = vf
    cnt = jnp.maximum(pos - (N_MASKED - 1), 0).astype(F32)

    for gi, w in enumerate(POOL_WINDOWS):
        sl = slice(gi * grp, (gi + 1) * grp)
        acc = vf[:, sl]
        for d in range(1, w):
            acc = acc + vbuf[POOL_HALO - d:POOL_HALO - d + tm, sl]
        cnt_prev = jnp.maximum(pos - w - (N_MASKED - 1), 0).astype(F32)
        n_win = jnp.maximum(cnt - cnt_prev, 1.0)
        u = acc / n_win - vf[:, sl]
        y = _dot(u.astype(BF16), pw_ref[gi])
        po_ref[:, sl] = (y * ps_ref[:, sl]).astype(po_ref.dtype)

    u = jnp.where(valid, cc_ref[...].astype(F32) * ch_ref[...].astype(F32), 0.0)
    ubuf[CONV_HALO:CONV_HALO + tm, :] = u
    y = u * cw_ref[CONV_WIDTH - 1:CONV_WIDTH, :]
    for d in range(1, CONV_WIDTH):
        y = y + ubuf[CONV_HALO - d:CONV_HALO - d + tm, :] * cw_ref[CONV_WIDTH - 1 - d:CONV_WIDTH - d, :]
    co_ref[...] = (cb_ref[...].astype(F32) * y).astype(co_ref.dtype)

    vbuf[0:POOL_HALO, :] = vbuf[tm:tm + POOL_HALO, :]
    ubuf[0:CONV_HALO, :] = ubuf[tm:tm + CONV_HALO, :]


def _local(proj3, pool_w, pool_scale, conv_w):
    bsz, lp, _ = proj3.shape
    w = pool_scale.shape[-1]
    tm = _pick_tile(lp, (768, 512, 256))

    def col(cb):
        return pl.BlockSpec((None, tm, w), lambda b, t: (b, t, cb))

    out = pl.BlockSpec((None, tm, w), lambda b, t: (b, t, 0))
    shp = jax.ShapeDtypeStruct((bsz, lp, w), BF16)
    return pl.pallas_call(
        _local_kernel,
        grid=(bsz, lp // tm),
        in_specs=[col(_C_POOL_V), col(_C_SC_H), col(_C_SC_B), col(_C_SC_C),
                  pl.BlockSpec(pool_w.shape, lambda b, t: (0, 0, 0)),
                  pl.BlockSpec((1, w), lambda b, t: (0, 0)),
                  pl.BlockSpec(conv_w.shape, lambda b, t: (0, 0))],
        out_specs=[out, out],
        out_shape=[shp, shp],
        scratch_shapes=[pltpu.VMEM((tm + POOL_HALO, w), F32), pltpu.VMEM((tm + CONV_HALO, w), F32)],
        compiler_params=_params("parallel", "arbitrary"),
        name="local",
    )(proj3, proj3, proj3, proj3, pool_w, pool_scale, conv_w)


def _attn_kernel(qt_ref, k_ref, vt_ref, o_ref, qm_ref, acc_ref, carry_ref, lb_ref, sp_ref):
    qi = pl.program_id(1)
    blk = ATT_BLK
    width = qt_ref.shape[0]
    n_pair = width // LANES
    dh = width // SB_HEADS

    acc_ref[...] = jnp.zeros_like(acc_ref)
    carry_ref[...] = jnp.zeros_like(carry_ref)
    zeros = jnp.zeros((dh, blk), BF16)
    for p in range(n_pair):
        qm_ref[2 * p] = jnp.concatenate([qt_ref[p * LANES:p * LANES + dh, :], zeros], axis=0)
        qm_ref[2 * p + 1] = jnp.concatenate([zeros, qt_ref[p * LANES + dh:(p + 1) * LANES, :]], axis=0)

    rr = lax.broadcasted_iota(jnp.int32, (blk, blk), 0)
    cc = lax.broadcasted_iota(jnp.int32, (blk, blk), 1)
    later = (cc > rr).astype(BF16)
    sign_bit = jnp.uint32(0x80000000)

    def step(score_blk, score_par, out_blk, out_par, diagonal=False):
        later_sums, scores = {}, {}

        def issue_later_sums(h):
            later_sums[h] = _dot(later, sp_ref[out_par, h])

        def issue_scores(h):
            ks = pl.multiple_of(score_blk * blk, blk)
            p = h // 2
            scores[h] = _dot(k_ref[pl.ds(ks, blk), p * LANES:(p + 1) * LANES], qm_ref[h])

        def finish_scores(h):
            z = scores[h]
            if diagonal:
                key_loc = lax.broadcasted_iota(jnp.int32, (blk, 1), 0)
                q_loc = lax.broadcasted_iota(jnp.int32, (1, blk), 1)
                z = jnp.where(key_loc < q_loc, z, MASKED_SCORE)
            neg_abs = lax.bitcast_convert_type(
                lax.bitcast_convert_type(z, jnp.uint32) | sign_bit, F32)
            sp = jnp.maximum(z, 0.0) + jnp.log2(1.0 + jnp.exp2(neg_abs))
            lb_ref[score_par, h] = z - sp
            sp_ref[score_par, h] = sp.astype(BF16)

        def finish_output(h):
            ks = pl.multiple_of(out_blk * blk, blk)
            p, hh = h // 2, h % 2
            later_sum = later_sums[h]
            a = jnp.exp2(lb_ref[out_par, h] - later_sum).astype(BF16)
            carry = carry_ref[h]
            first_sp = sp_ref[out_par, h, 0:1, :].astype(F32)
            carry_ref[h] = carry + (later_sum[0:1, :] + first_sp)
            vt = vt_ref[p * LANES:(p + 1) * LANES, pl.ds(ks, blk)]
            ot = _dot(vt, a)
            scale = jnp.tile(jnp.exp2(-carry), (dh // 8, 1))
            acc_ref[p, hh * dh:(hh + 1) * dh, :] += ot[hh * dh:(hh + 1) * dh, :] * scale

        for t in range(SB_HEADS + ATT_LOOKAHEAD):
            if t < SB_HEADS:
                if out_blk is not None:
                    issue_later_sums(t)
                if score_blk is not None:
                    issue_scores(t)
            h = t - ATT_LOOKAHEAD
            if h >= 0:
                if out_blk is not None:
                    finish_output(h)
                if score_blk is not None:
                    finish_scores(h)

    step(qi, 0, None, None, diagonal=True)

    def keep_walking(state):
        i, min_carry = state
        return jnp.logical_and(i <= qi, min_carry < ZERO_WEIGHT_CARRY)

    def walk(state):
        i, _ = state
        for par in range(2):
            @pl.when((i & 1) == par)
            def _():
                step(qi - i, par, qi - i + 1, 1 - par)
        return i + 1, jnp.min(carry_ref[...])

    _, min_carry = lax.while_loop(keep_walking, walk, (jnp.int32(1), jnp.float32(0.0)))

    for par in range(2):
        @pl.when(jnp.logical_and((qi & 1) == par, min_carry < ZERO_WEIGHT_CARRY))
        def _():
            step(None, None, 0, par)

    for p in range(n_pair):
        o_ref[:, p * LANES:(p + 1) * LANES] = acc_ref[p].T.astype(o_ref.dtype)


def _attn(qt, k, vt):
    bsz, w, lp = qt.shape
    return pl.pallas_call(
        _attn_kernel,
        grid=(bsz, lp // ATT_BLK),
        in_specs=[pl.BlockSpec((None, w, ATT_BLK), lambda b, i: (b, 0, i)),
                  pl.BlockSpec((None, lp, w), lambda b, i: (b, 0, 0)),
                  pl.BlockSpec((None, w, lp), lambda b, i: (b, 0, 0))],
        out_specs=pl.BlockSpec((None, ATT_BLK, w), lambda b, i: (b, i, 0)),
        out_shape=jax.ShapeDtypeStruct((bsz, lp, w), BF16),
        scratch_shapes=[pltpu.VMEM((SB_HEADS, LANES, ATT_BLK), BF16),
                        pltpu.VMEM((w // LANES, LANES, ATT_BLK), F32),
                        pltpu.VMEM((SB_HEADS, 8, ATT_BLK), F32),
                        pltpu.VMEM((2, SB_HEADS, ATT_BLK, ATT_BLK), F32),
                        pltpu.VMEM((2, SB_HEADS, ATT_BLK, ATT_BLK), BF16)],
        compiler_params=_params("parallel", "arbitrary"),
        name="attn",
    )(qt, k, vt)


def _merge_kernel(x_ref, g_ref, b0_ref, b1_ref, b2_ref, b3_ref, wb_ref, wo_ref, o_ref):
    d = x_ref.shape[1]
    mixed = jnp.zeros(x_ref.shape, F32)
    for n, b_ref in enumerate((b0_ref, b1_ref, b2_ref, b3_ref)):
        y = _dot(b_ref[...], wb_ref[n])
        mixed = mixed + jax.nn.sigmoid(g_ref[:, n * d:(n + 1) * d].astype(F32)) * y
    o_ref[...] = x_ref[...] + _dot(mixed.astype(BF16), wo_ref[...])


def _merge(x2d, proj2d, branches, w_branch, w_o):
    n, d = x2d.shape
    w = branches[0].shape[-1]
    tm = _pick_tile(n, (512, 256, 128))
    row = lambda i: (i, 0)
    const1 = pl.Buffered(1)
    return pl.pallas_call(
        _merge_kernel,
        grid=(n // tm,),
        in_specs=[pl.BlockSpec((tm, d), row),
                  pl.BlockSpec((tm, N_BRANCH * d), row)] +
                 [pl.BlockSpec((tm, w), row)] * N_BRANCH +
                 [pl.BlockSpec(w_branch.shape, lambda i: (0, 0, 0), pipeline_mode=const1),
                  pl.BlockSpec(w_o.shape, lambda i: (0, 0), pipeline_mode=const1)],
        out_specs=pl.BlockSpec((tm, d), row),
        out_shape=jax.ShapeDtypeStruct((n, d), F32),
        compiler_params=_params("parallel"),
        name="merge",
    )(x2d, proj2d, *branches, w_branch, w_o)


def _mlp_kernel(x_ref, g_ref, wu_ref, wd_ref, fg_ref, o_ref, *, final_norm, ff_chunk):
    x = x_ref[...]
    h = _rms(x, g_ref[...]).astype(BF16)
    acc = x
    for c in range(wu_ref.shape[1] // ff_chunk):
        a = jnp.maximum(_dot(h, wu_ref[:, c * ff_chunk:(c + 1) * ff_chunk]), 0.0)
        acc = acc + _dot((a * a).astype(BF16), wd_ref[c * ff_chunk:(c + 1) * ff_chunk, :])
    if final_norm:
        acc = _rms(acc, fg_ref[...])
    o_ref[...] = acc


def _mlp(x2d, g, w_up, w_down, final_g, final_norm):
    n, d = x2d.shape
    tm = _pick_tile(n, (512, 256, 128))
    row = lambda i: (i, 0)
    vec = pl.BlockSpec((1, d), lambda i: (0, 0))
    const1 = pl.Buffered(1)
    return pl.pallas_call(
        functools.partial(_mlp_kernel, final_norm=final_norm, ff_chunk=1024),
        grid=(n // tm,),
        in_specs=[pl.BlockSpec((tm, d), row), vec,
                  pl.BlockSpec(w_up.shape, lambda i: (0, 0), pipeline_mode=const1),
                  pl.BlockSpec(w_down.shape, lambda i: (0, 0), pipeline_mode=const1),
                  vec],
        out_specs=pl.BlockSpec((tm, d), row),
        out_shape=jax.ShapeDtypeStruct((n, d), F32),
        compiler_params=_params("parallel"),
        name="mlp",
    )(x2d, g, w_up, w_down, final_g)


def kernel(x, meta_tokens, lb_logits, norm1_g, w_in, hg_norm_g, pool_w, pool_scale, conv_w,
           w_branch, w_o, norm2_g, w_up, w_down, final_norm_g):
    bsz, seq, d = x.shape
    depth = w_in.shape[0]
    w = d // 2
    assert w % LANES == 0 and w // HG_HEADS == BLOCK and w // len(POOL_WINDOWS) == LANES
    assert w_in.shape[2] == N_SPLIT * w + N_BRANCH * d

    length = FRONT + seq
    lp = -(-length // ATT_BLK) * ATT_BLK
    lead = jnp.concatenate([jnp.zeros((N_MASKED, d), x.dtype), meta_tokens.astype(x.dtype)], axis=0)
    h = jnp.concatenate([jnp.broadcast_to(lead[None], (bsz, FRONT, d)), x,
                         jnp.zeros((bsz, lp - length, d), x.dtype)], axis=1)
    h = h.reshape(bsz * lp, d)

    cum = jnp.cumsum(jax.nn.softmax(lb_logits.astype(F32), axis=0), axis=0)
    lower_bounds = cum - cum[0]

    w_main = jnp.concatenate([w_in[:, :, N_SPLIT * w:], w_in[:, :, :5 * w], w_in[:, :, 8 * w:N_SPLIT * w]],
                             axis=2).astype(BF16)
    w_qkv = w_in[:, :, 5 * w:8 * w].astype(BF16)
    w_branch16, w_o16 = w_branch.astype(BF16), w_o.astype(BF16)
    w_up16, w_down16 = w_up.astype(BF16), w_down.astype(BF16)
    pool_w16 = pool_w.astype(BF16)
    fg = final_norm_g.reshape(1, d)

    for layer in range(depth):
        g1 = norm1_g[layer].reshape(1, d)
        proj = _inproj(h, g1, w_main[layer])
        qt, k, vt = _qkvproj(h.reshape(bsz, lp, d), g1, w_qkv[layer])
        proj3 = proj.reshape(bsz, lp, -1)
        b_hg = _hgrn2(proj3, lower_bounds[layer].reshape(1, w), hg_norm_g[layer].reshape(1, w))
        b_pool, b_conv = _local(proj3, pool_w16[layer], pool_scale[layer].reshape(1, w), conv_w[layer])
        b_sb = _attn(qt, k, vt)
        branches = [b.reshape(bsz * lp, w) for b in (b_hg, b_pool, b_sb, b_conv)]
        h = _merge(h, proj, branches, w_branch16[layer], w_o16[layer])
        h = _mlp(h, norm2_g[layer].reshape(1, d), w_up16[layer], w_down16[layer], fg,
                 final_norm=(layer == depth - 1))

    return h.reshape(bsz, lp, d)[:, FRONT:length]
```

```python
import functools
import math

import jax
import jax.numpy as jnp
from jax import lax
from jax.experimental import pallas as pl
from jax.experimental.pallas import tpu as pltpu

F32 = jnp.float32
BF16 = jnp.bfloat16

N_META = 16
BLOCK = 128
FRONT = BLOCK
N_MASKED = FRONT - N_META
N_BRANCH = 4
HG_HEADS = 4
SB_HEADS = 8
POOL_WINDOWS = (2, 4, 8, 16)
CONV_WIDTH = 3
N_SPLIT = 11
EPS = 1e-6

LANES = 128
SUB = 8
ATT_BLK = 256
ATT_LOOKAHEAD = 2
ZERO_WEIGHT_CARRY = 160.0
MASKED_SCORE = -1e30
VMEM_LIMIT = 52 * 1024 * 1024

_GATE_BLOCKS = 2 * N_BRANCH
(_C_HG_Q, _C_HG_F, _C_HG_I, _C_HG_G, _C_POOL_V, _C_SC_H, _C_SC_B, _C_SC_C) = range(
    _GATE_BLOCKS, _GATE_BLOCKS + 8)


def _pick_tile(n, candidates):
    for c in candidates:
        if n % c == 0:
            return c
    raise ValueError(f"no tile in {candidates} divides {n}")


def _params(*sem):
    return pltpu.CompilerParams(dimension_semantics=sem, vmem_limit_bytes=VMEM_LIMIT)


def _rms(x, g):
    return x * lax.rsqrt(jnp.mean(x * x, axis=-1, keepdims=True) + EPS) * g


def _dot(a, b):
    return jnp.dot(a, b, preferred_element_type=F32)


def _dot_nt(a, b):
    return lax.dot_general(a, b, (((1,), (1,)), ((), ())), preferred_element_type=F32)


def _dot_split(m_bf16, x):
    hi = x.astype(BF16)
    lo = (x - hi.astype(F32)).astype(BF16)
    return _dot(m_bf16, hi) + _dot(m_bf16, lo)


def _inproj_kernel(x_ref, g_ref, w_ref, o_ref, h_ref):
    @pl.when(pl.program_id(1) == 0)
    def _():
        h_ref[...] = _rms(x_ref[...], g_ref[...]).astype(BF16)

    o_ref[...] = _dot(h_ref[...], w_ref[...]).astype(o_ref.dtype)


def _inproj(x2d, g, w):
    n, d = x2d.shape
    cols = w.shape[1]
    tm = _pick_tile(n, (768, 512, 256, 128))
    tn = _pick_tile(cols, (2048, 1024, 512))
    return pl.pallas_call(
        _inproj_kernel,
        grid=(n // tm, cols // tn),
        in_specs=[pl.BlockSpec((tm, d), lambda i, j: (i, 0)),
                  pl.BlockSpec((1, d), lambda i, j: (0, 0)),
                  pl.BlockSpec((d, tn), lambda i, j: (0, j))],
        out_specs=pl.BlockSpec((tm, tn), lambda i, j: (i, j)),
        out_shape=jax.ShapeDtypeStruct((n, cols), BF16),
        scratch_shapes=[pltpu.VMEM((tm, d), BF16)],
        compiler_params=_params("parallel", "arbitrary"),
        name="inproj",
    )(x2d, g, w)


def _qkvproj_kernel(x_ref, g_ref, w_ref, qt_ref, k_ref, vt_ref, *, q_scale):
    w = k_ref.shape[1]
    h = _rms(x_ref[...], g_ref[...]).astype(BF16)
    y = _dot(h, w_ref[...])
    qt_ref[...] = (y[:, 0:w] * q_scale).T.astype(BF16)
    k_ref[...] = y[:, w:2 * w].astype(BF16)
    pos = lax.broadcasted_iota(jnp.int32, (x_ref.shape[0], 1), 0) + pl.program_id(1) * x_ref.shape[0]
    vt_ref[...] = jnp.where(pos >= N_MASKED, y[:, 2 * w:3 * w], 0.0).T.astype(BF16)


def _qkvproj(x3d, g, w_qkv):
    bsz, lp, d = x3d.shape
    w = w_qkv.shape[1] // 3
    tm = _pick_tile(lp, (768, 512, 256))
    q_scale = math.log2(math.e) / math.sqrt(w // SB_HEADS)
    row = pl.BlockSpec((None, tm, w), lambda b, t: (b, t, 0))
    colm = pl.BlockSpec((None, w, tm), lambda b, t: (b, 0, t))
    return pl.pallas_call(
        functools.partial(_qkvproj_kernel, q_scale=q_scale),
        grid=(bsz, lp // tm),
        in_specs=[pl.BlockSpec((None, tm, d), lambda b, t: (b, t, 0)),
                  pl.BlockSpec((1, d), lambda b, t: (0, 0)),
                  pl.BlockSpec(w_qkv.shape, lambda b, t: (0, 0))],
        out_specs=[colm, row, colm],
        out_shape=[jax.ShapeDtypeStruct((bsz, w, lp), BF16),
                   jax.ShapeDtypeStruct((bsz, lp, w), BF16),
                   jax.ShapeDtypeStruct((bsz, w, lp), BF16)],
        compiler_params=_params("parallel", "parallel"),
        name="qkvproj",
    )(x3d, g, w_qkv)


HG_LEVELS = (64, 32, 16, 8, 4, 2, 1)
LOG2E = math.log2(math.e)


def _hgrn2_kernel(q_ref, f_ref, i_ref, g_ref, lb_ref, ng_ref, o_ref, st_ref):
    step_i = pl.program_id(1)
    tm, width = q_ref.shape
    dk = BLOCK
    tiles = BLOCK // SUB

    @pl.when(step_i == 0)
    def _():
        st_ref[...] = jnp.zeros_like(st_ref)

    r2 = lax.broadcasted_iota(jnp.int32, (BLOCK, BLOCK), 0)
    c2 = lax.broadcasted_iota(jnp.int32, (BLOCK, BLOCK), 1)
    tril = (c2 <= r2).astype(BF16)
    differ = r2 ^ c2
    level = jnp.zeros((BLOCK, BLOCK), jnp.int32)
    for c in reversed(HG_LEVELS):
        level = jnp.where(differ >= c, c, level)
    level = jnp.where(c2 > r2, -1, level)
    sub_row = lax.broadcasted_iota(jnp.int32, (1, SUB, 1), 1)
    sign_bit = jnp.uint32(0x80000000)
    lb = lb_ref[...]

    def chunk(ci, carry):
        r0 = pl.multiple_of(ci * BLOCK, BLOCK)
        rows = pl.ds(r0, BLOCK)
        pos = lax.broadcasted_iota(jnp.int32, (BLOCK, 1), 0) + (step_i * tm + r0)
        valid = pos >= N_MASKED
        xf = f_ref[rows, :].astype(F32)
        log_f = jnp.where(valid, jnp.log(lb + (1.0 - lb) * jax.nn.sigmoid(xf)), 0.0)
        kk = jnp.where(valid, (1.0 - lb) * jax.nn.sigmoid(-xf), 0.0)
        vv = jnp.where(valid, i_ref[rows, :].astype(F32), 0.0)
        qq = q_ref[rows, :].astype(F32)
        b2 = _dot_split(tril, log_f) * LOG2E

        issued = {}

        def issue(h):
            sl = slice(h * dk, (h + 1) * dk)
            q, k, v, b = qq[:, sl], kk[:, sl], vv[:, sl], b2[:, sl]
            b3 = b.reshape(tiles, SUB, dk)

            def tile_row(r):
                return jnp.broadcast_to(b3[:, r:r + 1, :], b3.shape)

            last3 = tile_row(SUB - 1)
            scores = {0: _dot_nt(q.astype(BF16), k.astype(BF16))}
            for c in HG_LEVELS:
                if c >= SUB:
                    t8 = c // SUB
                    idx = [(t // t8) * t8 - 1 if (t // t8) % 2 else (t // t8 + 1) * t8 - 1
                           for t in range(tiles)]
                    ref3 = jnp.concatenate([last3[i:i + 1] for i in idx], axis=0)
                elif c == 4:
                    ref3 = tile_row(3)
                elif c == 2:
                    ref3 = jnp.where(sub_row < 4, tile_row(1), tile_row(5))
                else:
                    ref3 = jnp.where(sub_row % 2 == 1, pltpu.roll(b3, 1, axis=1), b3)
                d = b - ref3.reshape(BLOCK, dk)
                g = jnp.exp2(lax.bitcast_convert_type(lax.bitcast_convert_type(d, jnp.uint32) | sign_bit, F32))
                scores[c] = _dot_nt((q * g).astype(BF16), (k * g).astype(BF16))

            b_last = b[BLOCK - 1:BLOCK, :]
            st = st_ref[h]
            o_inter = _dot_nt((q * jnp.exp2(b)).astype(BF16), st.astype(BF16))
            k_out = (k * jnp.exp2(b_last - b)).astype(BF16)
            st_ref[h] = st * jnp.exp2(b_last) + _dot(v.T.astype(BF16), k_out)
            issued[h] = (scores, o_inter, v.astype(BF16))

        def finish(h):
            sl = slice(h * dk, (h + 1) * dk)
            scores, o_inter, v16 = issued.pop(h)
            p = jnp.zeros((BLOCK, BLOCK), F32)
            for c in (0,) + HG_LEVELS:
                p = jnp.where(level == c, scores[c], p)
            o = o_inter + _dot(p.astype(BF16), v16)
            o = o * lax.rsqrt(jnp.mean(o * o, axis=-1, keepdims=True) + EPS) * ng_ref[:, sl]
            o_ref[rows, sl] = (o * jax.nn.sigmoid(g_ref[rows, sl].astype(F32))).astype(o_ref.dtype)

        for h in range(HG_HEADS + 1):
            if h < HG_HEADS:
                issue(h)
            if h >= 1:
                finish(h - 1)
        return carry

    lax.fori_loop(0, tm // BLOCK, chunk, 0)


def _hgrn2(proj3, lb, ng):
    bsz, lp, _ = proj3.shape
    w = lb.shape[-1]
    tm = _pick_tile(lp, (768, 512, 256))

    def col(cb):
        return pl.BlockSpec((None, tm, w), lambda b, c: (b, c, cb))

    vec = pl.BlockSpec((1, w), lambda b, c: (0, 0))
    return pl.pallas_call(
        _hgrn2_kernel,
        grid=(bsz, lp // tm),
        in_specs=[col(_C_HG_Q), col(_C_HG_F), col(_C_HG_I), col(_C_HG_G), vec, vec],
        out_specs=pl.BlockSpec((None, tm, w), lambda b, c: (b, c, 0)),
        out_shape=jax.ShapeDtypeStruct((bsz, lp, w), BF16),
        scratch_shapes=[pltpu.VMEM((HG_HEADS, BLOCK, BLOCK), F32)],
        compiler_params=_params("parallel", "arbitrary"),
        name="hgrn2",
    )(proj3, proj3, proj3, proj3, lb, ng)


POOL_HALO = 16
CONV_HALO = 8


def _local_kernel(pv_ref, ch_ref, cb_ref, cc_ref, pw_ref, ps_ref, cw_ref, po_ref, co_ref,
                  vbuf, ubuf):
    t = pl.program_id(1)
    tm = pv_ref.shape[0]
    grp = pv_ref.shape[1] // len(POOL_WINDOWS)

    @pl.when(t == 0)
    def _():
        vbuf[0:POOL_HALO, :] = jnp.zeros((POOL_HALO, vbuf.shape[1]), F32)
        ubuf[0:CONV_HALO, :] = jnp.zeros((CONV_HALO, ubuf.shape[1]), F32)

    pos = lax.broadcasted_iota(jnp.int32, (tm, 1), 0) + t * tm
    valid = pos >= N_MASKED
    vf = jnp.where(valid, pv_ref[...].astype(F32), 0.0)
    vbuf[POOL_HALO:POOL_HALO + tm, :] = vf
    cnt = jnp.maximum(pos - (N_MASKED - 1), 0).astype(F32)

    for gi, w in enumerate(POOL_WINDOWS):
        sl = slice(gi * grp, (gi + 1) * grp)
        acc = vf[:, sl]
        for d in range(1, w):
            acc = acc + vbuf[POOL_HALO - d:POOL_HALO - d + tm, sl]
        cnt_prev = jnp.maximum(pos - w - (N_MASKED - 1), 0).astype(F32)
        n_win = jnp.maximum(cnt - cnt_prev, 1.0)
        u = acc / n_win - vf[:, sl]
        y = _dot(u.astype(BF16), pw_ref[gi])
        po_ref[:, sl] = (y * ps_ref[:, sl]).astype(po_ref.dtype)

    u = jnp.where(valid, cc_ref[...].astype(F32) * ch_ref[...].astype(F32), 0.0)
    ubuf[CONV_HALO:CONV_HALO + tm, :] = u
    y = u * cw_ref[CONV_WIDTH - 1:CONV_WIDTH, :]
    for d in range(1, CONV_WIDTH):
        y = y + ubuf[CONV_HALO - d:CONV_HALO - d + tm, :] * cw_ref[CONV_WIDTH - 1 - d:CONV_WIDTH - d, :]
    co_ref[...] = (cb_ref[...].astype(F32) * y).astype(co_ref.dtype)

    vbuf[0:POOL_HALO, :] = vbuf[tm:tm + POOL_HALO, :]
    ubuf[0:CONV_HALO, :] = ubuf[tm:tm + CONV_HALO, :]


def _local(proj3, pool_w, pool_scale, conv_w):
    bsz, lp, _ = proj3.shape
    w = pool_scale.shape[-1]
    tm = _pick_tile(lp, (768, 512, 256))

    def col(cb):
        return pl.BlockSpec((None, tm, w), lambda b, t: (b, t, cb))

    out = pl.BlockSpec((None, tm, w), lambda b, t: (b, t, 0))
    shp = jax.ShapeDtypeStruct((bsz, lp, w), BF16)
    return pl.pallas_call(
        _local_kernel,
        grid=(bsz, lp // tm),
        in_specs=[col(_C_POOL_V), col(_C_SC_H), col(_C_SC_B), col(_C_SC_C),
                  pl.BlockSpec(pool_w.shape, lambda b, t: (0, 0, 0)),
                  pl.BlockSpec((1, w), lambda b, t: (0, 0)),
                  pl.BlockSpec(conv_w.shape, lambda b, t: (0, 0))],
        out_specs=[out, out],
        out_shape=[shp, shp],
        scratch_shapes=[pltpu.VMEM((tm + POOL_HALO, w), F32), pltpu.VMEM((tm + CONV_HALO, w), F32)],
        compiler_params=_params("parallel", "arbitrary"),
        name="local",
    )(proj3, proj3, proj3, proj3, pool_w, pool_scale, conv_w)


def _attn_kernel(qt_ref, k_ref, vt_ref, o_ref, qm_ref, acc_ref, carry_ref, lb_ref, sp_ref):
    qi = pl.program_id(1)
    blk = ATT_BLK
    width = qt_ref.shape[0]
    n_pair = width // LANES
    dh = width // SB_HEADS

    acc_ref[...] = jnp.zeros_like(acc_ref)
    carry_ref[...] = jnp.zeros_like(carry_ref)
    zeros = jnp.zeros((dh, blk), BF16)
    for p in range(n_pair):
        qm_ref[2 * p] = jnp.concatenate([qt_ref[p * LANES:p * LANES + dh, :], zeros], axis=0)
        qm_ref[2 * p + 1] = jnp.concatenate([zeros, qt_ref[p * LANES + dh:(p + 1) * LANES, :]], axis=0)

    rr = lax.broadcasted_iota(jnp.int32, (blk, blk), 0)
    cc = lax.broadcasted_iota(jnp.int32, (blk, blk), 1)
    later = (cc > rr).astype(BF16)
    sign_bit = jnp.uint32(0x80000000)

    def step(score_blk, score_par, out_blk, out_par, diagonal=False):
        later_sums, scores = {}, {}

        def issue_later_sums(h):
            later_sums[h] = _dot(later, sp_ref[out_par, h])

        def issue_scores(h):
            ks = pl.multiple_of(score_blk * blk, blk)
            p = h // 2
            scores[h] = _dot(k_ref[pl.ds(ks, blk), p * LANES:(p + 1) * LANES], qm_ref[h])

        def finish_scores(h):
            z = scores[h]
            if diagonal:
                key_loc = lax.broadcasted_iota(jnp.int32, (blk, 1), 0)
                q_loc = lax.broadcasted_iota(jnp.int32, (1, blk), 1)
                z = jnp.where(key_loc < q_loc, z, MASKED_SCORE)
            neg_abs = lax.bitcast_convert_type(
                lax.bitcast_convert_type(z, jnp.uint32) | sign_bit, F32)
            sp = jnp.maximum(z, 0.0) + jnp.log2(1.0 + jnp.exp2(neg_abs))
            lb_ref[score_par, h] = z - sp
            sp_ref[score_par, h] = sp.astype(BF16)

        def finish_output(h):
            ks = pl.multiple_of(out_blk * blk, blk)
            p, hh = h // 2, h % 2
            later_sum = later_sums[h]
            a = jnp.exp2(lb_ref[out_par, h] - later_sum).astype(BF16)
            carry = carry_ref[h]
            first_sp = sp_ref[out_par, h, 0:1, :].astype(F32)
            carry_ref[h] = carry + (later_sum[0:1, :] + first_sp)
            vt = vt_ref[p * LANES:(p + 1) * LANES, pl.ds(ks, blk)]
            ot = _dot(vt, a)
            scale = jnp.tile(jnp.exp2(-carry), (dh // 8, 1))
            acc_ref[p, hh * dh:(hh + 1) * dh, :] += ot[hh * dh:(hh + 1) * dh, :] * scale

        for t in range(SB_HEADS + ATT_LOOKAHEAD):
            if t < SB_HEADS:
                if out_blk is not None:
                    issue_later_sums(t)
                if score_blk is not None:
                    issue_scores(t)
            h = t - ATT_LOOKAHEAD
            if h >= 0:
                if out_blk is not None:
                    finish_output(h)
                if score_blk is not None:
                    finish_scores(h)

    step(qi, 0, None, None, diagonal=True)

    def keep_walking(state):
        i, min_carry = state
        return jnp.logical_and(i <= qi, min_carry < ZERO_WEIGHT_CARRY)

    def walk(state):
        i, _ = state
        for par in range(2):
            @pl.when((i & 1) == par)
            def _():
                step(qi - i, par, qi - i + 1, 1 - par)
        return i + 1, jnp.min(carry_ref[...])

    _, min_carry = lax.while_loop(keep_walking, walk, (jnp.int32(1), jnp.float32(0.0)))

    for par in range(2):
        @pl.when(jnp.logical_and((qi & 1) == par, min_carry < ZERO_WEIGHT_CARRY))
        def _():
            step(None, None, 0, par)

    for p in range(n_pair):
        o_ref[:, p * LANES:(p + 1) * LANES] = acc_ref[p].T.astype(o_ref.dtype)


def _attn(qt, k, vt):
    bsz, w, lp = qt.shape
    return pl.pallas_call(
        _attn_kernel,
        grid=(bsz, lp // ATT_BLK),
        in_specs=[pl.BlockSpec((None, w, ATT_BLK), lambda b, i: (b, 0, i)),
                  pl.BlockSpec((None, lp, w), lambda b, i: (b, 0, 0)),
                  pl.BlockSpec((None, w, lp), lambda b, i: (b, 0, 0))],
        out_specs=pl.BlockSpec((None, ATT_BLK, w), lambda b, i: (b, i, 0)),
        out_shape=jax.ShapeDtypeStruct((bsz, lp, w), BF16),
        scratch_shapes=[pltpu.VMEM((SB_HEADS, LANES, ATT_BLK), BF16),
                        pltpu.VMEM((w // LANES, LANES, ATT_BLK), F32),
                        pltpu.VMEM((SB_HEADS, 8, ATT_BLK), F32),
                        pltpu.VMEM((2, SB_HEADS, ATT_BLK, ATT_BLK), F32),
                        pltpu.VMEM((2, SB_HEADS, ATT_BLK, ATT_BLK), BF16)],
        compiler_params=_params("parallel", "arbitrary"),
        name="attn",
    )(qt, k, vt)


FF_CHUNK = 1024


def _mixmlp_kernel(x_ref, gate_ref, b0_ref, b1_ref, b2_ref, b3_ref, wb_ref, wo_ref, g2_ref, wu_ref, wd_ref,
                   fg_ref, o_ref, *, final_norm):
    d = x_ref.shape[1]
    mixed = jnp.zeros(x_ref.shape, F32)
    for n, b_ref in enumerate((b0_ref, b1_ref, b2_ref, b3_ref)):
        y = _dot(b_ref[...], wb_ref[n])
        mixed = mixed + jax.nn.sigmoid(gate_ref[:, n * d:(n + 1) * d].astype(F32)) * y
    x = x_ref[...] + _dot(mixed.astype(BF16), wo_ref[...])

    h = _rms(x, g2_ref[...]).astype(BF16)
    acc = x
    for c in range(wu_ref.shape[1] // FF_CHUNK):
        a = jnp.maximum(_dot(h, wu_ref[:, c * FF_CHUNK:(c + 1) * FF_CHUNK]), 0.0)
        acc = acc + _dot((a * a).astype(BF16), wd_ref[c * FF_CHUNK:(c + 1) * FF_CHUNK, :])
    if final_norm:
        acc = _rms(acc, fg_ref[...])
    o_ref[...] = acc


def _mixmlp(x2d, proj2d, branches, w_branch, w_o, g2, w_up, w_down, final_g, final_norm):
    n, d = x2d.shape
    w = branches[0].shape[-1]
    tm = _pick_tile(n, (512, 256, 128))
    row = lambda i: (i, 0)
    vec = pl.BlockSpec((1, d), lambda i: (0, 0))
    const1 = pl.Buffered(1)
    return pl.pallas_call(
        functools.partial(_mixmlp_kernel, final_norm=final_norm),
        grid=(n // tm,),
        in_specs=[pl.BlockSpec((tm, d), row),
                  pl.BlockSpec((tm, N_BRANCH * d), row)] +
                 [pl.BlockSpec((tm, w), row)] * N_BRANCH +
                 [pl.BlockSpec(w_branch.shape, lambda i: (0, 0, 0), pipeline_mode=const1),
                  pl.BlockSpec(w_o.shape, lambda i: (0, 0), pipeline_mode=const1),
                  vec,
                  pl.BlockSpec(w_up.shape, lambda i: (0, 0), pipeline_mode=const1),
                  pl.BlockSpec(w_down.shape, lambda i: (0, 0), pipeline_mode=const1),
                  vec],
        out_specs=pl.BlockSpec((tm, d), row),
        out_shape=jax.ShapeDtypeStruct((n, d), F32),
        compiler_params=_params("parallel"),
        name="mixmlp",
    )(x2d, proj2d, *branches, w_branch, w_o, g2, w_up, w_down, final_g)


def kernel(x, meta_tokens, lb_logits, norm1_g, w_in, hg_norm_g, pool_w, pool_scale, conv_w,
           w_branch, w_o, norm2_g, w_up, w_down, final_norm_g):
    bsz, seq, d = x.shape
    depth = w_in.shape[0]
    w = d // 2
    assert w % LANES == 0 and w // HG_HEADS == BLOCK and w // len(POOL_WINDOWS) == LANES
    assert w_in.shape[2] == N_SPLIT * w + N_BRANCH * d

    length = FRONT + seq
    lp = -(-length // ATT_BLK) * ATT_BLK
    h = jnp.pad(x, ((0, 0), (FRONT, lp - length), (0, 0)))
    h = lax.dynamic_update_slice(
        h, jnp.broadcast_to(meta_tokens.astype(x.dtype)[None], (bsz, N_META, d)), (0, N_MASKED, 0))
    h = h.reshape(bsz * lp, d)

    cum = jnp.cumsum(jax.nn.softmax(lb_logits.astype(F32), axis=0), axis=0)
    lower_bounds = cum - cum[0]

    w_main = jnp.concatenate([w_in[:, :, N_SPLIT * w:], w_in[:, :, :5 * w], w_in[:, :, 8 * w:N_SPLIT * w]],
                             axis=2).astype(BF16)
    w_qkv = w_in[:, :, 5 * w:8 * w].astype(BF16)
    w_branch16, w_o16 = w_branch.astype(BF16), w_o.astype(BF16)
    w_up16, w_down16 = w_up.astype(BF16), w_down.astype(BF16)
    pool_w16 = pool_w.astype(BF16)
    fg = final_norm_g.reshape(1, d)

    for layer in range(depth):
        g1 = norm1_g[layer].reshape(1, d)
        proj = _inproj(h, g1, w_main[layer])
        qt, k, vt = _qkvproj(h.reshape(bsz, lp, d), g1, w_qkv[layer])
        proj3 = proj.reshape(bsz, lp, -1)
        b_hg = _hgrn2(proj3, lower_bounds[layer].reshape(1, w), hg_norm_g[layer].reshape(1, w))
        b_pool, b_conv = _local(proj3, pool_w16[layer], pool_scale[layer].reshape(1, w), conv_w[layer])
        b_sb = _attn(qt, k, vt)
        branches = [b.reshape(bsz * lp, w) for b in (b_hg, b_pool, b_sb, b_conv)]
        h = _mixmlp(h, proj, branches, w_branch16[layer], w_o16[layer], norm2_g[layer].reshape(1, d),
                    w_up16[layer], w_down16[layer], fg, final_norm=(layer == depth - 1))

    return h.reshape(bsz, lp, d)[:, FRONT:length]
```

```python
import functools
import math

import jax
import jax.numpy as jnp
from jax import lax
from jax.experimental import pallas as pl
from jax.experimental.pallas import tpu as pltpu

F32 = jnp.float32
BF16 = jnp.bfloat16

N_META = 16
BLOCK = 128
FRONT = BLOCK
N_MASKED = FRONT - N_META
N_BRANCH = 4
HG_HEADS = 4
SB_HEADS = 8
POOL_WINDOWS = (2, 4, 8, 16)
CONV_WIDTH = 3
N_SPLIT = 11
EPS = 1e-6

LANES = 128
SUB = 8
ATT_BLK = 256
ATT_LOOKAHEAD = 2
ZERO_WEIGHT_CARRY = 160.0
MASKED_SCORE = -1e30
VMEM_LIMIT = 52 * 1024 * 1024

_GATE_BLOCKS = 2 * N_BRANCH
(_C_HG_Q, _C_HG_F, _C_HG_I, _C_HG_G, _C_POOL_V, _C_SC_H, _C_SC_B, _C_SC_C) = range(
    _GATE_BLOCKS, _GATE_BLOCKS + 8)


def _pick_tile(n, candidates):
    for c in candidates:
        if n % c == 0:
            return c
    raise ValueError(f"no tile in {candidates} divides {n}")


def _params(*sem):
    return pltpu.CompilerParams(dimension_semantics=sem, vmem_limit_bytes=VMEM_LIMIT)


def _rms(x, g):
    return x * lax.rsqrt(jnp.mean(x * x, axis=-1, keepdims=True) + EPS) * g


def _dot(a, b):
    return jnp.dot(a, b, preferred_element_type=F32)


def _dot_nt(a, b):
    return lax.dot_general(a, b, (((1,), (1,)), ((), ())), preferred_element_type=F32)


def _dot_split(m_bf16, x):
    hi = x.astype(BF16)
    lo = (x - hi.astype(F32)).astype(BF16)
    return _dot(m_bf16, hi) + _dot(m_bf16, lo)


def _inproj_kernel(x_ref, g_ref, w_ref, o_ref, h_ref):
    @pl.when(pl.program_id(1) == 0)
    def _():
        h_ref[...] = _rms(x_ref[...], g_ref[...]).astype(BF16)

    o_ref[...] = _dot(h_ref[...], w_ref[...]).astype(o_ref.dtype)


def _inproj(x2d, g, w_all, layer):
    n, d = x2d.shape
    cols = w_all.shape[2]
    tm = _pick_tile(n, (1536, 768, 512, 256, 128))
    tn = _pick_tile(cols, (2048, 1024, 512))
    return pl.pallas_call(
        _inproj_kernel,
        grid=(n // tm, cols // tn),
        in_specs=[pl.BlockSpec((tm, d), lambda i, j: (i, 0)),
                  pl.BlockSpec((1, d), lambda i, j: (0, 0)),
                  pl.BlockSpec((None, d, tn), lambda i, j: (layer, 0, j))],
        out_specs=pl.BlockSpec((tm, tn), lambda i, j: (i, j)),
        out_shape=jax.ShapeDtypeStruct((n, cols), BF16),
        scratch_shapes=[pltpu.VMEM((tm, d), BF16)],
        compiler_params=_params("parallel", "arbitrary"),
        name="inproj",
    )(x2d, g, w_all)


def _qkvproj_kernel(x_ref, g_ref, w_ref, qt_ref, k_ref, vt_ref, *, q_scale):
    w = k_ref.shape[1]
    h = _rms(x_ref[...], g_ref[...]).astype(BF16)
    y = _dot(h, w_ref[...])
    qt_ref[...] = (y[:, 0:w] * q_scale).T.astype(BF16)
    k_ref[...] = y[:, w:2 * w].astype(BF16)
    pos = lax.broadcasted_iota(jnp.int32, (x_ref.shape[0], 1), 0) + pl.program_id(1) * x_ref.shape[0]
    vt_ref[...] = jnp.where(pos >= N_MASKED, y[:, 2 * w:3 * w], 0.0).T.astype(BF16)


def _qkvproj(x3d, g, w_qkv, layer):
    bsz, lp, d = x3d.shape
    w = w_qkv.shape[2] // 3
    tm = _pick_tile(lp, (768, 512, 256))
    q_scale = math.log2(math.e) / math.sqrt(w // SB_HEADS)
    row = pl.BlockSpec((None, tm, w), lambda b, t: (b, t, 0))
    colm = pl.BlockSpec((None, w, tm), lambda b, t: (b, 0, t))
    return pl.pallas_call(
        functools.partial(_qkvproj_kernel, q_scale=q_scale),
        grid=(bsz, lp // tm),
        in_specs=[pl.BlockSpec((None, tm, d), lambda b, t: (b, t, 0)),
                  pl.BlockSpec((1, d), lambda b, t: (0, 0)),
                  pl.BlockSpec((None,) + w_qkv.shape[1:], lambda b, t: (layer, 0, 0))],
        out_specs=[colm, row, colm],
        out_shape=[jax.ShapeDtypeStruct((bsz, w, lp), BF16),
                   jax.ShapeDtypeStruct((bsz, lp, w), BF16),
                   jax.ShapeDtypeStruct((bsz, w, lp), BF16)],
        compiler_params=_params("parallel", "parallel"),
        name="qkvproj",
    )(x3d, g, w_qkv)


HG_LEVELS = (64, 32, 16, 8, 4, 2, 1)
LOG2E = math.log2(math.e)


def _hgrn2_kernel(q_ref, f_ref, i_ref, g_ref, lb_ref, ng_ref, o_ref, st_ref):
    step_i = pl.program_id(1)
    tm, width = q_ref.shape
    dk = BLOCK
    tiles = BLOCK // SUB

    @pl.when(step_i == 0)
    def _():
        st_ref[...] = jnp.zeros_like(st_ref)

    r2 = lax.broadcasted_iota(jnp.int32, (BLOCK, BLOCK), 0)
    c2 = lax.broadcasted_iota(jnp.int32, (BLOCK, BLOCK), 1)
    tril = (c2 <= r2).astype(BF16)
    differ = r2 ^ c2
    level = jnp.zeros((BLOCK, BLOCK), jnp.int32)
    for c in reversed(HG_LEVELS):
        level = jnp.where(differ >= c, c, level)
    level = jnp.where(c2 > r2, -1, level)
    sub_row = lax.broadcasted_iota(jnp.int32, (1, SUB, 1), 1)
    sign_bit = jnp.uint32(0x80000000)
    lb = lb_ref[...]

    def chunk(ci, carry):
        r0 = pl.multiple_of(ci * BLOCK, BLOCK)
        rows = pl.ds(r0, BLOCK)
        pos = lax.broadcasted_iota(jnp.int32, (BLOCK, 1), 0) + (step_i * tm + r0)
        valid = pos >= N_MASKED
        xf = f_ref[rows, :].astype(F32)
        log_f = jnp.where(valid, jnp.log(lb + (1.0 - lb) * jax.nn.sigmoid(xf)), 0.0)
        kk = jnp.where(valid, (1.0 - lb) * jax.nn.sigmoid(-xf), 0.0)
        vv = jnp.where(valid, i_ref[rows, :].astype(F32), 0.0)
        qq = q_ref[rows, :].astype(F32)
        b2 = _dot_split(tril, log_f) * LOG2E

        issued = {}

        def issue(h):
            sl = slice(h * dk, (h + 1) * dk)
            q, k, v, b = qq[:, sl], kk[:, sl], vv[:, sl], b2[:, sl]
            b3 = b.reshape(tiles, SUB, dk)

            def tile_row(r):
                return jnp.broadcast_to(b3[:, r:r + 1, :], b3.shape)

            last3 = tile_row(SUB - 1)
            scores = {0: _dot_nt(q.astype(BF16), k.astype(BF16))}
            for c in HG_LEVELS:
                if c >= SUB:
                    t8 = c // SUB
                    idx = [(t // t8) * t8 - 1 if (t // t8) % 2 else (t // t8 + 1) * t8 - 1
                           for t in range(tiles)]
                    ref3 = jnp.concatenate([last3[i:i + 1] for i in idx], axis=0)
                elif c == 4:
                    ref3 = tile_row(3)
                elif c == 2:
                    ref3 = jnp.where(sub_row < 4, tile_row(1), tile_row(5))
                else:
                    ref3 = jnp.where(sub_row % 2 == 1, pltpu.roll(b3, 1, axis=1), b3)
                d = b - ref3.reshape(BLOCK, dk)
                g = jnp.exp2(lax.bitcast_convert_type(lax.bitcast_convert_type(d, jnp.uint32) | sign_bit, F32))
                scores[c] = _dot_nt((q * g).astype(BF16), (k * g).astype(BF16))

            b_last = b[BLOCK - 1:BLOCK, :]
            st = st_ref[h]
            o_inter = _dot_nt((q * jnp.exp2(b)).astype(BF16), st.astype(BF16))
            k_out = (k * jnp.exp2(b_last - b)).astype(BF16)
            st_ref[h] = st * jnp.exp2(b_last) + _dot(v.T.astype(BF16), k_out)
            issued[h] = (scores, o_inter, v.astype(BF16))

        def finish(h):
            sl = slice(h * dk, (h + 1) * dk)
            scores, o_inter, v16 = issued.pop(h)
            p = jnp.zeros((BLOCK, BLOCK), F32)
            for c in (0,) + HG_LEVELS:
                p = jnp.where(level == c, scores[c], p)
            o = o_inter + _dot(p.astype(BF16), v16)
            o = o * lax.rsqrt(jnp.mean(o * o, axis=-1, keepdims=True) + EPS) * ng_ref[:, sl]
            o_ref[rows, sl] = (o * jax.nn.sigmoid(g_ref[rows, sl].astype(F32))).astype(o_ref.dtype)

        for h in range(HG_HEADS + 1):
            if h < HG_HEADS:
                issue(h)
            if h >= 1:
                finish(h - 1)
        return carry

    lax.fori_loop(0, tm // BLOCK, chunk, 0)


def _hgrn2(proj3, lb, ng):
    bsz, lp, _ = proj3.shape
    w = lb.shape[-1]
    tm = _pick_tile(lp, (768, 512, 256))

    def col(cb):
        return pl.BlockSpec((None, tm, w), lambda b, c: (b, c, cb))

    vec = pl.BlockSpec((1, w), lambda b, c: (0, 0))
    return pl.pallas_call(
        _hgrn2_kernel,
        grid=(bsz, lp // tm),
        in_specs=[col(_C_HG_Q), col(_C_HG_F), col(_C_HG_I), col(_C_HG_G), vec, vec],
        out_specs=pl.BlockSpec((None, tm, w), lambda b, c: (b, c, 0)),
        out_shape=jax.ShapeDtypeStruct((bsz, lp, w), BF16),
        scratch_shapes=[pltpu.VMEM((HG_HEADS, BLOCK, BLOCK), F32)],
        compiler_params=_params("parallel", "arbitrary"),
        name="hgrn2",
    )(proj3, proj3, proj3, proj3, lb, ng)


POOL_HALO = 16
CONV_HALO = 8


def _local_kernel(pv_ref, ch_ref, cb_ref, cc_ref, pw_ref, ps_ref, cw_ref, po_ref, co_ref,
                  vbuf, ubuf):
    t = pl.program_id(1)
    tm = pv_ref.shape[0]
    grp = pv_ref.shape[1] // len(POOL_WINDOWS)

    @pl.when(t == 0)
    def _():
        vbuf[0:POOL_HALO, :] = jnp.zeros((POOL_HALO, vbuf.shape[1]), F32)
        ubuf[0:CONV_HALO, :] = jnp.zeros((CONV_HALO, ubuf.shape[1]), F32)

    pos = lax.broadcasted_iota(jnp.int32, (tm, 1), 0) + t * tm
    valid = pos >= N_MASKED
    vf = jnp.where(valid, pv_ref[...].astype(F32), 0.0)
    vbuf[POOL_HALO:POOL_HALO + tm, :] = vf
    cnt = jnp.maximum(pos - (N_MASKED - 1), 0).astype(F32)

    for gi, w in enumerate(POOL_WINDOWS):
        sl = slice(gi * grp, (gi + 1) * grp)
        acc = vf[:, sl]
        for d in range(1, w):
            acc = acc + vbuf[POOL_HALO - d:POOL_HALO - d + tm, sl]
        cnt_prev = jnp.maximum(pos - w - (N_MASKED - 1), 0).astype(F32)
        n_win = jnp.maximum(cnt - cnt_prev, 1.0)
        u = acc / n_win - vf[:, sl]
        y = _dot(u.astype(BF16), pw_ref[gi])
        po_ref[:, sl] = (y * ps_ref[:, sl]).astype(po_ref.dtype)

    u = jnp.where(valid, cc_ref[...].astype(F32) * ch_ref[...].astype(F32), 0.0)
    ubuf[CONV_HALO:CONV_HALO + tm, :] = u
    y = u * cw_ref[CONV_WIDTH - 1:CONV_WIDTH, :]
    for d in range(1, CONV_WIDTH):
        y = y + ubuf[CONV_HALO - d:CONV_HALO - d + tm, :] * cw_ref[CONV_WIDTH - 1 - d:CONV_WIDTH - d, :]
    co_ref[...] = (cb_ref[...].astype(F32) * y).astype(co_ref.dtype)

    vbuf[0:POOL_HALO, :] = vbuf[tm:tm + POOL_HALO, :]
    ubuf[0:CONV_HALO, :] = ubuf[tm:tm + CONV_HALO, :]


def _local(proj3, pool_w, pool_scale, conv_w):
    bsz, lp, _ = proj3.shape
    w = pool_scale.shape[-1]
    tm = _pick_tile(lp, (768, 512, 256))

    def col(cb):
        return pl.BlockSpec((None, tm, w), lambda b, t: (b, t, cb))

    out = pl.BlockSpec((None, tm, w), lambda b, t: (b, t, 0))
    shp = jax.ShapeDtypeStruct((bsz, lp, w), BF16)
    return pl.pallas_call(
        _local_kernel,
        grid=(bsz, lp // tm),
        in_specs=[col(_C_POOL_V), col(_C_SC_H), col(_C_SC_B), col(_C_SC_C),
                  pl.BlockSpec(pool_w.shape, lambda b, t: (0, 0, 0)),
                  pl.BlockSpec((1, w), lambda b, t: (0, 0)),
                  pl.BlockSpec(conv_w.shape, lambda b, t: (0, 0))],
        out_specs=[out, out],
        out_shape=[shp, shp],
        scratch_shapes=[pltpu.VMEM((tm + POOL_HALO, w), F32), pltpu.VMEM((tm + CONV_HALO, w), F32)],
        compiler_params=_params("parallel", "arbitrary"),
        name="local",
    )(proj3, proj3, proj3, proj3, pool_w, pool_scale, conv_w)


def _attn_kernel(qt_ref, k_ref, vt_ref, o_ref, qm_ref, acc_ref, carry_ref, lb_ref, sp_ref):
    qi = pl.program_id(1)
    blk = ATT_BLK
    width = qt_ref.shape[0]
    n_pair = width // LANES
    dh = width // SB_HEADS

    acc_ref[...] = jnp.zeros_like(acc_ref)
    carry_ref[...] = jnp.zeros_like(carry_ref)
    zeros = jnp.zeros((dh, blk), BF16)
    for p in range(n_pair):
        qm_ref[2 * p] = jnp.concatenate([qt_ref[p * LANES:p * LANES + dh, :], zeros], axis=0)
        qm_ref[2 * p + 1] = jnp.concatenate([zeros, qt_ref[p * LANES + dh:(p + 1) * LANES, :]], axis=0)

    rr = lax.broadcasted_iota(jnp.int32, (blk, blk), 0)
    cc = lax.broadcasted_iota(jnp.int32, (blk, blk), 1)
    later = (cc > rr).astype(BF16)
    sign_bit = jnp.uint32(0x80000000)

    def step(score_blk, score_par, out_blk, out_par, diagonal=False):
        later_sums, scores = {}, {}

        def issue_later_sums(h):
            later_sums[h] = _dot(later, sp_ref[out_par, h])

        def issue_scores(h):
            ks = pl.multiple_of(score_blk * blk, blk)
            p = h // 2
            scores[h] = _dot(k_ref[pl.ds(ks, blk), p * LANES:(p + 1) * LANES], qm_ref[h])

        def finish_scores(h):
            z = scores[h]
            if diagonal:
                key_loc = lax.broadcasted_iota(jnp.int32, (blk, 1), 0)
                q_loc = lax.broadcasted_iota(jnp.int32, (1, blk), 1)
                z = jnp.where(key_loc < q_loc, z, MASKED_SCORE)
            neg_abs = lax.bitcast_convert_type(
                lax.bitcast_convert_type(z, jnp.uint32) | sign_bit, F32)
            sp = jnp.maximum(z, 0.0) + jnp.log2(1.0 + jnp.exp2(neg_abs))
            lb_ref[score_par, h] = z - sp
            sp_ref[score_par, h] = sp.astype(BF16)

        def finish_output(h):
            ks = pl.multiple_of(out_blk * blk, blk)
            p, hh = h // 2, h % 2
            later_sum = later_sums[h]
            a = jnp.exp2(lb_ref[out_par, h] - later_sum).astype(BF16)
            carry = carry_ref[h]
            first_sp = sp_ref[out_par, h, 0:1, :].astype(F32)
            carry_ref[h] = carry + (later_sum[0:1, :] + first_sp)
            vt = vt_ref[p * LANES:(p + 1) * LANES, pl.ds(ks, blk)]
            ot = _dot(vt, a)
            scale = jnp.tile(jnp.exp2(-carry), (dh // 8, 1))
            acc_ref[p, hh * dh:(hh + 1) * dh, :] += ot[hh * dh:(hh + 1) * dh, :] * scale

        for t in range(SB_HEADS + ATT_LOOKAHEAD):
            if t < SB_HEADS:
                if out_blk is not None:
                    issue_later_sums(t)
                if score_blk is not None:
                    issue_scores(t)
            h = t - ATT_LOOKAHEAD
            if h >= 0:
                if out_blk is not None:
                    finish_output(h)
                if score_blk is not None:
                    finish_scores(h)

    step(qi, 0, None, None, diagonal=True)

    def keep_walking(state):
        i, min_carry = state
        return jnp.logical_and(i <= qi, min_carry < ZERO_WEIGHT_CARRY)

    def walk(state):
        i, _ = state
        for par in range(2):
            @pl.when((i & 1) == par)
            def _():
                step(qi - i, par, qi - i + 1, 1 - par)
        return i + 1, jnp.min(carry_ref[...])

    _, min_carry = lax.while_loop(keep_walking, walk, (jnp.int32(1), jnp.float32(0.0)))

    for par in range(2):
        @pl.when(jnp.logical_and((qi & 1) == par, min_carry < ZERO_WEIGHT_CARRY))
        def _():
            step(None, None, 0, par)

    for p in range(n_pair):
        o_ref[:, p * LANES:(p + 1) * LANES] = acc_ref[p].T.astype(o_ref.dtype)


def _attn(qt, k, vt):
    bsz, w, lp = qt.shape
    return pl.pallas_call(
        _attn_kernel,
        grid=(bsz, lp // ATT_BLK),
        in_specs=[pl.BlockSpec((None, w, ATT_BLK), lambda b, i: (b, 0, i)),
                  pl.BlockSpec((None, lp, w), lambda b, i: (b, 0, 0)),
                  pl.BlockSpec((None, w, lp), lambda b, i: (b, 0, 0))],
        out_specs=pl.BlockSpec((None, ATT_BLK, w), lambda b, i: (b, i, 0)),
        out_shape=jax.ShapeDtypeStruct((bsz, lp, w), BF16),
        scratch_shapes=[pltpu.VMEM((SB_HEADS, LANES, ATT_BLK), BF16),
                        pltpu.VMEM((w // LANES, LANES, ATT_BLK), F32),
                        pltpu.VMEM((SB_HEADS, 8, ATT_BLK), F32),
                        pltpu.VMEM((2, SB_HEADS, ATT_BLK, ATT_BLK), F32),
                        pltpu.VMEM((2, SB_HEADS, ATT_BLK, ATT_BLK), BF16)],
        compiler_params=_params("parallel", "arbitrary"),
        name="attn",
    )(qt, k, vt)


FF_CHUNK = 1024


def _mixmlp_kernel(x_ref, gate_ref, b0_ref, b1_ref, b2_ref, b3_ref, wb_ref, wo_ref, g2_ref, wu_ref, wd_ref,
                   fg_ref, o_ref, *, final_norm):
    d = x_ref.shape[1]
    mixed = jnp.zeros(x_ref.shape, F32)
    for n, b_ref in enumerate((b0_ref, b1_ref, b2_ref, b3_ref)):
        y = _dot(b_ref[...], wb_ref[n])
        mixed = mixed + jax.nn.sigmoid(gate_ref[:, n * d:(n + 1) * d].astype(F32)) * y
    x = x_ref[...] + _dot(mixed.astype(BF16), wo_ref[...])

    h = _rms(x, g2_ref[...]).astype(BF16)
    acc = x
    for c in range(wu_ref.shape[1] // FF_CHUNK):
        a = jnp.maximum(_dot(h, wu_ref[:, c * FF_CHUNK:(c + 1) * FF_CHUNK]), 0.0)
        acc = acc + _dot((a * a).astype(BF16), wd_ref[c * FF_CHUNK:(c + 1) * FF_CHUNK, :])
    if final_norm:
        acc = _rms(acc, fg_ref[...])
    o_ref[...] = acc


def _mixmlp(x2d, proj2d, branches, w_branch, w_o, g2, w_up, w_down, final_g, final_norm, layer):
    n, d = x2d.shape
    w = branches[0].shape[-1]
    tm = _pick_tile(n, (512, 256, 128))
    row = lambda i: (i, 0)
    vec = pl.BlockSpec((1, d), lambda i: (0, 0))
    const1 = pl.Buffered(1)

    def layer_weight(a):
        zeros = (0,) * (a.ndim - 1)
        return pl.BlockSpec((None,) + a.shape[1:], lambda i: (layer,) + zeros, pipeline_mode=const1)

    return pl.pallas_call(
        functools.partial(_mixmlp_kernel, final_norm=final_norm),
        grid=(n // tm,),
        in_specs=[pl.BlockSpec((tm, d), row),
                  pl.BlockSpec((tm, N_BRANCH * d), row)] +
                 [pl.BlockSpec((tm, w), row)] * N_BRANCH +
                 [layer_weight(w_branch), layer_weight(w_o), vec, layer_weight(w_up), layer_weight(w_down),
                  vec],
        out_specs=pl.BlockSpec((tm, d), row),
        out_shape=jax.ShapeDtypeStruct((n, d), F32),
        compiler_params=_params("parallel"),
        name="mixmlp",
    )(x2d, proj2d, *branches, w_branch, w_o, g2, w_up, w_down, final_g)


def kernel(x, meta_tokens, lb_logits, norm1_g, w_in, hg_norm_g, pool_w, pool_scale, conv_w,
           w_branch, w_o, norm2_g, w_up, w_down, final_norm_g):
    bsz, seq, d = x.shape
    depth = w_in.shape[0]
    w = d // 2
    assert w % LANES == 0 and w // HG_HEADS == BLOCK and w // len(POOL_WINDOWS) == LANES
    assert w_in.shape[2] == N_SPLIT * w + N_BRANCH * d

    length = FRONT + seq
    lp = -(-length // ATT_BLK) * ATT_BLK
    h = lax.dynamic_update_slice(jnp.zeros((bsz, lp, d), x.dtype), x, (0, FRONT, 0))
    h = lax.dynamic_update_slice(
        h, jnp.broadcast_to(meta_tokens.astype(x.dtype)[None], (bsz, N_META, d)), (0, N_MASKED, 0))
    h = h.reshape(bsz * lp, d)

    cum = jnp.cumsum(jax.nn.softmax(lb_logits.astype(F32), axis=0), axis=0)
    lower_bounds = cum - cum[0]

    w_in16 = w_in.astype(BF16)
    w_main = jnp.concatenate([w_in16[:, :, N_SPLIT * w:], w_in16[:, :, :5 * w], w_in16[:, :, 8 * w:N_SPLIT * w]],
                             axis=2)
    w_qkv = w_in16[:, :, 5 * w:8 * w]
    w_branch16, w_o16 = w_branch.astype(BF16), w_o.astype(BF16)
    w_up16, w_down16 = w_up.astype(BF16), w_down.astype(BF16)
    pool_w16 = pool_w.astype(BF16)
    fg = final_norm_g.reshape(1, d)

    for layer in range(depth):
        g1 = norm1_g[layer].reshape(1, d)
        proj = _inproj(h, g1, w_main, layer)
        qt, k, vt = _qkvproj(h.reshape(bsz, lp, d), g1, w_qkv, layer)
        proj3 = proj.reshape(bsz, lp, -1)
        b_hg = _hgrn2(proj3, lower_bounds[layer].reshape(1, w), hg_norm_g[layer].reshape(1, w))
        b_pool, b_conv = _local(proj3, pool_w16[layer], pool_scale[layer].reshape(1, w), conv_w[layer])
        b_sb = _attn(qt, k, vt)
        branches = [b.reshape(bsz * lp, w) for b in (b_hg, b_pool, b_sb, b_conv)]
        h = _mixmlp(h, proj, branches, w_branch16, w_o16, norm2_g[layer].reshape(1, d),
                    w_up16, w_down16, fg, final_norm=(layer == depth - 1), layer=layer)

    return h.reshape(bsz, lp, d)[:, FRONT:length]
```

```python
import functools
import math

import jax
import jax.numpy as jnp
from jax import lax
from jax.experimental import pallas as pl
from jax.experimental.pallas import tpu as pltpu

F32 = jnp.float32
BF16 = jnp.bfloat16

N_META = 16
BLOCK = 128
FRONT = BLOCK
N_MASKED = FRONT - N_META
N_BRANCH = 4
HG_HEADS = 4
SB_HEADS = 8
POOL_WINDOWS = (2, 4, 8, 16)
CONV_WIDTH = 3
N_SPLIT = 11
EPS = 1e-6

LANES = 128
SUB = 8
ATT_BLK = 256
ATT_LOOKAHEAD = 2
ZERO_WEIGHT_CARRY = 160.0
MASKED_SCORE = -1e30
VMEM_LIMIT = 52 * 1024 * 1024

_GATE_BLOCKS = 2 * N_BRANCH
(_C_HG_Q, _C_HG_F, _C_HG_I, _C_HG_G, _C_POOL_V, _C_SC_H, _C_SC_B, _C_SC_C) = range(
    _GATE_BLOCKS, _GATE_BLOCKS + 8)


def _pick_tile(n, candidates):
    for c in candidates:
        if n % c == 0:
            return c
    raise ValueError(f"no tile in {candidates} divides {n}")


def _params(*sem):
    return pltpu.CompilerParams(dimension_semantics=sem, vmem_limit_bytes=VMEM_LIMIT)


def _rms(x, g):
    return x * lax.rsqrt(jnp.mean(x * x, axis=-1, keepdims=True) + EPS) * g


def _dot(a, b):
    return jnp.dot(a, b, preferred_element_type=F32)


def _dot_nt(a, b):
    return lax.dot_general(a, b, (((1,), (1,)), ((), ())), preferred_element_type=F32)


def _dot_split(m_bf16, x):
    hi = x.astype(BF16)
    lo = (x - hi.astype(F32)).astype(BF16)
    return _dot(m_bf16, hi) + _dot(m_bf16, lo)


def _inproj_kernel(x_ref, g_ref, w_ref, o_ref, h_ref):
    @pl.when(pl.program_id(1) == 0)
    def _():
        h_ref[...] = _rms(x_ref[...], g_ref[...]).astype(BF16)

    o_ref[...] = _dot(h_ref[...], w_ref[...]).astype(o_ref.dtype)


def _inproj(x2d, g, w_all, layer):
    n, d = x2d.shape
    cols = w_all.shape[2]
    tm = _pick_tile(n, (1536, 768, 512, 256, 128))
    tn = _pick_tile(cols, (2048, 1024, 512))
    return pl.pallas_call(
        _inproj_kernel,
        grid=(n // tm, cols // tn),
        in_specs=[pl.BlockSpec((tm, d), lambda i, j: (i, 0)),
                  pl.BlockSpec((1, d), lambda i, j: (0, 0)),
                  pl.BlockSpec((None, d, tn), lambda i, j: (layer, 0, j))],
        out_specs=pl.BlockSpec((tm, tn), lambda i, j: (i, j)),
        out_shape=jax.ShapeDtypeStruct((n, cols), BF16),
        scratch_shapes=[pltpu.VMEM((tm, d), BF16)],
        compiler_params=_params("parallel", "arbitrary"),
        name="inproj",
    )(x2d, g, w_all)


def _qkvproj_kernel(x_ref, g_ref, w_ref, qt_ref, k_ref, vt_ref, *, q_scale):
    w = k_ref.shape[1]
    h = _rms(x_ref[...], g_ref[...]).astype(BF16)
    y = _dot(h, w_ref[...])
    qt_ref[...] = (y[:, 0:w] * q_scale).T.astype(BF16)
    k_ref[...] = y[:, w:2 * w].astype(BF16)
    pos = lax.broadcasted_iota(jnp.int32, (x_ref.shape[0], 1), 0) + pl.program_id(1) * x_ref.shape[0]
    vt_ref[...] = jnp.where(pos >= N_MASKED, y[:, 2 * w:3 * w], 0.0).T.astype(BF16)


def _qkvproj(x3d, g, w_qkv, layer):
    bsz, lp, d = x3d.shape
    w = w_qkv.shape[2] // 3
    tm = _pick_tile(lp, (768, 512, 256))
    q_scale = math.log2(math.e) / math.sqrt(w // SB_HEADS)
    row = pl.BlockSpec((None, tm, w), lambda b, t: (b, t, 0))
    colm = pl.BlockSpec((None, w, tm), lambda b, t: (b, 0, t))
    return pl.pallas_call(
        functools.partial(_qkvproj_kernel, q_scale=q_scale),
        grid=(bsz, lp // tm),
        in_specs=[pl.BlockSpec((None, tm, d), lambda b, t: (b, t, 0)),
                  pl.BlockSpec((1, d), lambda b, t: (0, 0)),
                  pl.BlockSpec((None,) + w_qkv.shape[1:], lambda b, t: (layer, 0, 0))],
        out_specs=[colm, row, colm],
        out_shape=[jax.ShapeDtypeStruct((bsz, w, lp), BF16),
                   jax.ShapeDtypeStruct((bsz, lp, w), BF16),
                   jax.ShapeDtypeStruct((bsz, w, lp), BF16)],
        compiler_params=_params("parallel", "parallel"),
        name="qkvproj",
    )(x3d, g, w_qkv)


HG_LEVELS = (64, 32, 16, 8, 4, 2, 1)
LOG2E = math.log2(math.e)


def _hgrn2_kernel(q_ref, f_ref, i_ref, g_ref, lb_ref, ng_ref, o_ref, st_ref):
    step_i = pl.program_id(1)
    tm, width = q_ref.shape
    dk = BLOCK
    tiles = BLOCK // SUB

    @pl.when(step_i == 0)
    def _():
        st_ref[...] = jnp.zeros_like(st_ref)

    r2 = lax.broadcasted_iota(jnp.int32, (BLOCK, BLOCK), 0)
    c2 = lax.broadcasted_iota(jnp.int32, (BLOCK, BLOCK), 1)
    tril = (c2 <= r2).astype(BF16)
    differ = r2 ^ c2
    level = jnp.zeros((BLOCK, BLOCK), jnp.int32)
    for c in reversed(HG_LEVELS):
        level = jnp.where(differ >= c, c, level)
    level = jnp.where(c2 > r2, -1, level)
    sub_row = lax.broadcasted_iota(jnp.int32, (1, SUB, 1), 1)
    sign_bit = jnp.uint32(0x80000000)
    lb = lb_ref[...]

    def chunk(ci, carry):
        r0 = pl.multiple_of(ci * BLOCK, BLOCK)
        rows = pl.ds(r0, BLOCK)
        pos = lax.broadcasted_iota(jnp.int32, (BLOCK, 1), 0) + (step_i * tm + r0)
        valid = pos >= N_MASKED
        xf = f_ref[rows, :].astype(F32)
        log_f = jnp.where(valid, jnp.log(lb + (1.0 - lb) * jax.nn.sigmoid(xf)), 0.0)
        kk = jnp.where(valid, (1.0 - lb) * jax.nn.sigmoid(-xf), 0.0)
        vv = jnp.where(valid, i_ref[rows, :].astype(F32), 0.0)
        qq = q_ref[rows, :].astype(F32)
        b2 = _dot_split(tril, log_f) * LOG2E

        issued = {}

        def issue(h):
            sl = slice(h * dk, (h + 1) * dk)
            q, k, v, b = qq[:, sl], kk[:, sl], vv[:, sl], b2[:, sl]
            b3 = b.reshape(tiles, SUB, dk)

            def tile_row(r):
                return jnp.broadcast_to(b3[:, r:r + 1, :], b3.shape)

            last3 = tile_row(SUB - 1)
            scores = {0: _dot_nt(q.astype(BF16), k.astype(BF16))}
            for c in HG_LEVELS:
                if c >= SUB:
                    t8 = c // SUB
                    idx = [(t // t8) * t8 - 1 if (t // t8) % 2 else (t // t8 + 1) * t8 - 1
                           for t in range(tiles)]
                    ref3 = jnp.concatenate([last3[i:i + 1] for i in idx], axis=0)
                elif c == 4:
                    ref3 = tile_row(3)
                elif c == 2:
                    ref3 = jnp.where(sub_row < 4, tile_row(1), tile_row(5))
                else:
                    ref3 = jnp.where(sub_row % 2 == 1, pltpu.roll(b3, 1, axis=1), b3)
                d = b - ref3.reshape(BLOCK, dk)
                g = jnp.exp2(lax.bitcast_convert_type(lax.bitcast_convert_type(d, jnp.uint32) | sign_bit, F32))
                scores[c] = _dot_nt((q * g).astype(BF16), (k * g).astype(BF16))

            b_last = b[BLOCK - 1:BLOCK, :]
            st = st_ref[h]
            o_inter = _dot_nt((q * jnp.exp2(b)).astype(BF16), st.astype(BF16))
            k_out = (k * jnp.exp2(b_last - b)).astype(BF16)
            st_ref[h] = st * jnp.exp2(b_last) + _dot(v.T.astype(BF16), k_out)
            issued[h] = (scores, o_inter, v.astype(BF16))

        def finish(h):
            sl = slice(h * dk, (h + 1) * dk)
            scores, o_inter, v16 = issued.pop(h)
            p = jnp.zeros((BLOCK, BLOCK), F32)
            for c in (0,) + HG_LEVELS:
                p = jnp.where(level == c, scores[c], p)
            o = o_inter + _dot(p.astype(BF16), v16)
            o = o * lax.rsqrt(jnp.mean(o * o, axis=-1, keepdims=True) + EPS) * ng_ref[:, sl]
            o_ref[rows, sl] = (o * jax.nn.sigmoid(g_ref[rows, sl].astype(F32))).astype(o_ref.dtype)

        for h in range(HG_HEADS + 1):
            if h < HG_HEADS:
                issue(h)
            if h >= 1:
                finish(h - 1)
        return carry

    lax.fori_loop(0, tm // BLOCK, chunk, 0)


def _hgrn2(proj3, lb, ng):
    bsz, lp, _ = proj3.shape
    w = lb.shape[-1]
    tm = _pick_tile(lp, (768, 512, 256))

    def col(cb):
        return pl.BlockSpec((None, tm, w), lambda b, c: (b, c, cb))

    vec = pl.BlockSpec((1, w), lambda b, c: (0, 0))
    return pl.pallas_call(
        _hgrn2_kernel,
        grid=(bsz, lp // tm),
        in_specs=[col(_C_HG_Q), col(_C_HG_F), col(_C_HG_I), col(_C_HG_G), vec, vec],
        out_specs=pl.BlockSpec((None, tm, w), lambda b, c: (b, c, 0)),
        out_shape=jax.ShapeDtypeStruct((bsz, lp, w), BF16),
        scratch_shapes=[pltpu.VMEM((HG_HEADS, BLOCK, BLOCK), F32)],
        compiler_params=_params("parallel", "arbitrary"),
        name="hgrn2",
    )(proj3, proj3, proj3, proj3, lb, ng)


POOL_HALO = 16
CONV_HALO = 8


def _local_kernel(pv_ref, ch_ref, cb_ref, cc_ref, pw_ref, ps_ref, cw_ref, po_ref, co_ref,
                  vbuf, ubuf):
    t = pl.program_id(1)
    tm = pv_ref.shape[0]
    grp = pv_ref.shape[1] // len(POOL_WINDOWS)

    @pl.when(t == 0)
    def _():
        vbuf[0:POOL_HALO, :] = jnp.zeros((POOL_HALO, vbuf.shape[1]), F32)
        ubuf[0:CONV_HALO, :] = jnp.zeros((CONV_HALO, ubuf.shape[1]), F32)

    pos = lax.broadcasted_iota(jnp.int32, (tm, 1), 0) + t * tm
    valid = pos >= N_MASKED
    vf = jnp.where(valid, pv_ref[...].astype(F32), 0.0)
    vbuf[POOL_HALO:POOL_HALO + tm, :] = vf
    cnt = jnp.maximum(pos - (N_MASKED - 1), 0).astype(F32)

    for gi, w in enumerate(POOL_WINDOWS):
        sl = slice(gi * grp, (gi + 1) * grp)
        acc = vf[:, sl]
        for d in range(1, w):
            acc = acc + vbuf[POOL_HALO - d:POOL_HALO - d + tm, sl]
        cnt_prev = jnp.maximum(pos - w - (N_MASKED - 1), 0).astype(F32)
        n_win = jnp.maximum(cnt - cnt_prev, 1.0)
        u = acc / n_win - vf[:, sl]
        y = _dot(u.astype(BF16), pw_ref[gi])
        po_ref[:, sl] = (y * ps_ref[:, sl]).astype(po_ref.dtype)

    u = jnp.where(valid, cc_ref[...].astype(F32) * ch_ref[...].astype(F32), 0.0)
    ubuf[CONV_HALO:CONV_HALO + tm, :] = u
    y = u * cw_ref[CONV_WIDTH - 1:CONV_WIDTH, :]
    for d in range(1, CONV_WIDTH):
        y = y + ubuf[CONV_HALO - d:CONV_HALO - d + tm, :] * cw_ref[CONV_WIDTH - 1 - d:CONV_WIDTH - d, :]
    co_ref[...] = (cb_ref[...].astype(F32) * y).astype(co_ref.dtype)

    vbuf[0:POOL_HALO, :] = vbuf[tm:tm + POOL_HALO, :]
    ubuf[0:CONV_HALO, :] = ubuf[tm:tm + CONV_HALO, :]


def _local(proj3, pool_w, pool_scale, conv_w):
    bsz, lp, _ = proj3.shape
    w = pool_scale.shape[-1]
    tm = _pick_tile(lp, (768, 512, 256))

    def col(cb):
        return pl.BlockSpec((None, tm, w), lambda b, t: (b, t, cb))

    out = pl.BlockSpec((None, tm, w), lambda b, t: (b, t, 0))
    shp = jax.ShapeDtypeStruct((bsz, lp, w), BF16)
    return pl.pallas_call(
        _local_kernel,
        grid=(bsz, lp // tm),
        in_specs=[col(_C_POOL_V), col(_C_SC_H), col(_C_SC_B), col(_C_SC_C),
                  pl.BlockSpec(pool_w.shape, lambda b, t: (0, 0, 0)),
                  pl.BlockSpec((1, w), lambda b, t: (0, 0)),
                  pl.BlockSpec(conv_w.shape, lambda b, t: (0, 0))],
        out_specs=[out, out],
        out_shape=[shp, shp],
        scratch_shapes=[pltpu.VMEM((tm + POOL_HALO, w), F32), pltpu.VMEM((tm + CONV_HALO, w), F32)],
        compiler_params=_params("parallel", "arbitrary"),
        name="local",
    )(proj3, proj3, proj3, proj3, pool_w, pool_scale, conv_w)


def _attn_kernel(qt_ref, k_ref, vt_ref, o_ref, qm_ref, acc_ref, carry_ref, lb_ref, sp_ref):
    qi = pl.program_id(1)
    blk = ATT_BLK
    width = qt_ref.shape[0]
    n_pair = width // LANES
    dh = width // SB_HEADS

    acc_ref[...] = jnp.zeros_like(acc_ref)
    carry_ref[...] = jnp.zeros_like(carry_ref)
    zeros = jnp.zeros((dh, blk), BF16)
    for p in range(n_pair):
        qm_ref[2 * p] = jnp.concatenate([qt_ref[p * LANES:p * LANES + dh, :], zeros], axis=0)
        qm_ref[2 * p + 1] = jnp.concatenate([zeros, qt_ref[p * LANES + dh:(p + 1) * LANES, :]], axis=0)

    rr = lax.broadcasted_iota(jnp.int32, (blk, blk), 0)
    cc = lax.broadcasted_iota(jnp.int32, (blk, blk), 1)
    later = (cc > rr).astype(BF16)
    sign_bit = jnp.uint32(0x80000000)

    def step(score_blk, score_par, out_blk, out_par, diagonal=False):
        later_sums, scores = {}, {}

        def issue_later_sums(h):
            later_sums[h] = _dot(later, sp_ref[out_par, h])

        def issue_scores(h):
            ks = pl.multiple_of(score_blk * blk, blk)
            p = h // 2
            scores[h] = _dot(k_ref[pl.ds(ks, blk), p * LANES:(p + 1) * LANES], qm_ref[h])

        def finish_scores(h):
            z = scores[h]
            if diagonal:
                key_loc = lax.broadcasted_iota(jnp.int32, (blk, 1), 0)
                q_loc = lax.broadcasted_iota(jnp.int32, (1, blk), 1)
                z = jnp.where(key_loc < q_loc, z, MASKED_SCORE)
            neg_abs = lax.bitcast_convert_type(
                lax.bitcast_convert_type(z, jnp.uint32) | sign_bit, F32)
            sp = jnp.maximum(z, 0.0) + jnp.log2(1.0 + jnp.exp2(neg_abs))
            lb_ref[score_par, h] = z - sp
            sp_ref[score_par, h] = sp.astype(BF16)

        def finish_output(h):
            ks = pl.multiple_of(out_blk * blk, blk)
            p, hh = h // 2, h % 2
            later_sum = later_sums[h]
            a = jnp.exp2(lb_ref[out_par, h] - later_sum).astype(BF16)
            carry = carry_ref[h]
            first_sp = sp_ref[out_par, h, 0:1, :].astype(F32)
            carry_ref[h] = carry + (later_sum[0:1, :] + first_sp)
            vt = vt_ref[p * LANES:(p + 1) * LANES, pl.ds(ks, blk)]
            ot = _dot(vt, a)
            scale = jnp.tile(jnp.exp2(-carry), (dh // 8, 1))
            acc_ref[p, hh * dh:(hh + 1) * dh, :] += ot[hh * dh:(hh + 1) * dh, :] * scale

        for t in range(SB_HEADS + ATT_LOOKAHEAD):
            if t < SB_HEADS:
                if out_blk is not None:
                    issue_later_sums(t)
                if score_blk is not None:
                    issue_scores(t)
            h = t - ATT_LOOKAHEAD
            if h >= 0:
                if out_blk is not None:
                    finish_output(h)
                if score_blk is not None:
                    finish_scores(h)

    step(qi, 0, None, None, diagonal=True)

    @pl.when(qi >= 1)
    def _():
        step(qi - 1, 1, qi, 0)

    def keep_walking(state):
        c, min_carry = state
        return jnp.logical_and(c >= 0, min_carry < ZERO_WEIGHT_CARRY)

    def walk(state):
        c, _ = state
        buf = (qi - c) & 1
        for par in range(2):
            @pl.when(buf == par)
            def _():
                step(None, None, c, par)
        min_carry = jnp.min(carry_ref[...])
        more = jnp.logical_and(c >= 1, min_carry < ZERO_WEIGHT_CARRY)
        for par in range(2):
            @pl.when(jnp.logical_and(more, buf == par))
            def _():
                step(c - 1, 1 - par, None, None)
        return c - 1, min_carry

    lax.while_loop(keep_walking, walk, (jnp.maximum(qi - 1, 0), jnp.float32(0.0)))

    for p in range(n_pair):
        o_ref[:, p * LANES:(p + 1) * LANES] = acc_ref[p].T.astype(o_ref.dtype)


def _attn(qt, k, vt):
    bsz, w, lp = qt.shape
    return pl.pallas_call(
        _attn_kernel,
        grid=(bsz, lp // ATT_BLK),
        in_specs=[pl.BlockSpec((None, w, ATT_BLK), lambda b, i: (b, 0, i)),
                  pl.BlockSpec((None, lp, w), lambda b, i: (b, 0, 0)),
                  pl.BlockSpec((None, w, lp), lambda b, i: (b, 0, 0))],
        out_specs=pl.BlockSpec((None, ATT_BLK, w), lambda b, i: (b, i, 0)),
        out_shape=jax.ShapeDtypeStruct((bsz, lp, w), BF16),
        scratch_shapes=[pltpu.VMEM((SB_HEADS, LANES, ATT_BLK), BF16),
                        pltpu.VMEM((w // LANES, LANES, ATT_BLK), F32),
                        pltpu.VMEM((SB_HEADS, 8, ATT_BLK), F32),
                        pltpu.VMEM((2, SB_HEADS, ATT_BLK, ATT_BLK), F32),
                        pltpu.VMEM((2, SB_HEADS, ATT_BLK, ATT_BLK), BF16)],
        compiler_params=_params("parallel", "arbitrary"),
        name="attn",
    )(qt, k, vt)


FF_CHUNK = 1024


def _mixmlp_kernel(x_ref, gate_ref, b0_ref, b1_ref, b2_ref, b3_ref, wb_ref, wo_ref, g2_ref, wu_ref, wd_ref,
                   fg_ref, o_ref, *, final_norm):
    d = x_ref.shape[1]
    mixed = jnp.zeros(x_ref.shape, F32)
    for n, b_ref in enumerate((b0_ref, b1_ref, b2_ref, b3_ref)):
        y = _dot(b_ref[...], wb_ref[n])
        mixed = mixed + jax.nn.sigmoid(gate_ref[:, n * d:(n + 1) * d].astype(F32)) * y
    x = x_ref[...] + _dot(mixed.astype(BF16), wo_ref[...])

    h = _rms(x, g2_ref[...]).astype(BF16)
    acc = x
    for c in range(wu_ref.shape[1] // FF_CHUNK):
        a = jnp.maximum(_dot(h, wu_ref[:, c * FF_CHUNK:(c + 1) * FF_CHUNK]), 0.0)
        acc = acc + _dot((a * a).astype(BF16), wd_ref[c * FF_CHUNK:(c + 1) * FF_CHUNK, :])
    if final_norm:
        acc = _rms(acc, fg_ref[...])
    o_ref[...] = acc


def _mixmlp(x2d, proj2d, branches, w_branch, w_o, g2, w_up, w_down, final_g, final_norm, layer):
    n, d = x2d.shape
    w = branches[0].shape[-1]
    tm = _pick_tile(n, (512, 256, 128))
    row = lambda i: (i, 0)
    vec = pl.BlockSpec((1, d), lambda i: (0, 0))
    const1 = pl.Buffered(1)

    def layer_weight(a):
        zeros = (0,) * (a.ndim - 1)
        return pl.BlockSpec((None,) + a.shape[1:], lambda i: (layer,) + zeros, pipeline_mode=const1)

    return pl.pallas_call(
        functools.partial(_mixmlp_kernel, final_norm=final_norm),
        grid=(n // tm,),
        in_specs=[pl.BlockSpec((tm, d), row),
                  pl.BlockSpec((tm, N_BRANCH * d), row)] +
                 [pl.BlockSpec((tm, w), row)] * N_BRANCH +
                 [layer_weight(w_branch), layer_weight(w_o), vec, layer_weight(w_up), layer_weight(w_down),
                  vec],
        out_specs=pl.BlockSpec((tm, d), row),
        out_shape=jax.ShapeDtypeStruct((n, d), F32),
        compiler_params=_params("parallel"),
        name="mixmlp",
    )(x2d, proj2d, *branches, w_branch, w_o, g2, w_up, w_down, final_g)


def kernel(x, meta_tokens, lb_logits, norm1_g, w_in, hg_norm_g, pool_w, pool_scale, conv_w,
           w_branch, w_o, norm2_g, w_up, w_down, final_norm_g):
    bsz, seq, d = x.shape
    depth = w_in.shape[0]
    w = d // 2
    assert w % LANES == 0 and w // HG_HEADS == BLOCK and w // len(POOL_WINDOWS) == LANES
    assert w_in.shape[2] == N_SPLIT * w + N_BRANCH * d

    length = FRONT + seq
    lp = -(-length // ATT_BLK) * ATT_BLK
    h = lax.dynamic_update_slice(jnp.zeros((bsz, lp, d), x.dtype), x, (0, FRONT, 0))
    h = lax.dynamic_update_slice(
        h, jnp.broadcast_to(meta_tokens.astype(x.dtype)[None], (bsz, N_META, d)), (0, N_MASKED, 0))
    h = h.reshape(bsz * lp, d)

    cum = jnp.cumsum(jax.nn.softmax(lb_logits.astype(F32), axis=0), axis=0)
    lower_bounds = cum - cum[0]

    w_in16 = w_in.astype(BF16)
    w_main = jnp.concatenate([w_in16[:, :, N_SPLIT * w:], w_in16[:, :, :5 * w], w_in16[:, :, 8 * w:N_SPLIT * w]],
                             axis=2)
    w_qkv = w_in16[:, :, 5 * w:8 * w]
    w_branch16, w_o16 = w_branch.astype(BF16), w_o.astype(BF16)
    w_up16, w_down16 = w_up.astype(BF16), w_down.astype(BF16)
    pool_w16 = pool_w.astype(BF16)
    fg = final_norm_g.reshape(1, d)

    for layer in range(depth):
        g1 = norm1_g[layer].reshape(1, d)
        proj = _inproj(h, g1, w_main, layer)
        qt, k, vt = _qkvproj(h.reshape(bsz, lp, d), g1, w_qkv, layer)
        proj3 = proj.reshape(bsz, lp, -1)
        b_hg = _hgrn2(proj3, lower_bounds[layer].reshape(1, w), hg_norm_g[layer].reshape(1, w))
        b_pool, b_conv = _local(proj3, pool_w16[layer], pool_scale[layer].reshape(1, w), conv_w[layer])
        b_sb = _attn(qt, k, vt)
        branches = [b.reshape(bsz * lp, w) for b in (b_hg, b_pool, b_sb, b_conv)]
        h = _mixmlp(h, proj, branches, w_branch16, w_o16, norm2_g[layer].reshape(1, d),
                    w_up16, w_down16, fg, final_norm=(layer == depth - 1), layer=layer)

    return h.reshape(bsz, lp, d)[:, FRONT:length]
```

```python
import functools
import math

import jax
import jax.numpy as jnp
from jax import lax
from jax.experimental import pallas as pl
from jax.experimental.pallas import tpu as pltpu

F32 = jnp.float32
BF16 = jnp.bfloat16

N_META = 16
BLOCK = 128
FRONT = BLOCK
N_MASKED = FRONT - N_META
N_BRANCH = 4
HG_HEADS = 4
SB_HEADS = 8
POOL_WINDOWS = (2, 4, 8, 16)
CONV_WIDTH = 3
N_SPLIT = 11
EPS = 1e-6

LANES = 128
SUB = 8
ATT_BLK = 256
ATT_LOOKAHEAD = 2
ZERO_WEIGHT_CARRY = 160.0
MASKED_SCORE = -1e30
VMEM_LIMIT = 52 * 1024 * 1024

_GATE_BLOCKS = 2 * N_BRANCH
(_C_HG_Q, _C_HG_F, _C_HG_I, _C_HG_G, _C_POOL_V, _C_SC_H, _C_SC_B, _C_SC_C) = range(
    _GATE_BLOCKS, _GATE_BLOCKS + 8)


def _pick_tile(n, candidates):
    for c in candidates:
        if n % c == 0:
            return c
    raise ValueError(f"no tile in {candidates} divides {n}")


def _params(*sem):
    return pltpu.CompilerParams(dimension_semantics=sem, vmem_limit_bytes=VMEM_LIMIT)


def _rms(x, g):
    return x * lax.rsqrt(jnp.mean(x * x, axis=-1, keepdims=True) + EPS) * g


def _dot(a, b):
    return jnp.dot(a, b, preferred_element_type=F32)


def _dot_nt(a, b):
    return lax.dot_general(a, b, (((1,), (1,)), ((), ())), preferred_element_type=F32)


def _dot_split(m_bf16, x):
    hi = x.astype(BF16)
    lo = (x - hi.astype(F32)).astype(BF16)
    return _dot(m_bf16, hi) + _dot(m_bf16, lo)


def _inproj_kernel(x_ref, g_ref, w_ref, o_ref, h_ref):
    @pl.when(pl.program_id(1) == 0)
    def _():
        h_ref[...] = _rms(x_ref[...], g_ref[...]).astype(BF16)

    o_ref[...] = _dot(h_ref[...], w_ref[...]).astype(o_ref.dtype)


def _inproj(x2d, g, w_all, layer):
    n, d = x2d.shape
    cols = w_all.shape[2]
    tm = _pick_tile(n, (768, 512, 256, 128))
    tn = _pick_tile(cols, (4096, 2048, 1024, 512))
    return pl.pallas_call(
        _inproj_kernel,
        grid=(n // tm, cols // tn),
        in_specs=[pl.BlockSpec((tm, d), lambda i, j: (i, 0)),
                  pl.BlockSpec((1, d), lambda i, j: (0, 0)),
                  pl.BlockSpec((None, d, tn), lambda i, j: (layer, 0, j))],
        out_specs=pl.BlockSpec((tm, tn), lambda i, j: (i, j)),
        out_shape=jax.ShapeDtypeStruct((n, cols), BF16),
        scratch_shapes=[pltpu.VMEM((tm, d), BF16)],
        compiler_params=_params("parallel", "arbitrary"),
        name="inproj",
    )(x2d, g, w_all)


def _qkvproj_kernel(x_ref, g_ref, w_ref, qt_ref, k_ref, vt_ref, *, q_scale):
    w = k_ref.shape[1]
    h = _rms(x_ref[...], g_ref[...]).astype(BF16)
    y = _dot(h, w_ref[...])
    qt_ref[...] = (y[:, 0:w] * q_scale).T.astype(BF16)
    k_ref[...] = y[:, w:2 * w].astype(BF16)
    pos = lax.broadcasted_iota(jnp.int32, (x_ref.shape[0], 1), 0) + pl.program_id(1) * x_ref.shape[0]
    vt_ref[...] = jnp.where(pos >= N_MASKED, y[:, 2 * w:3 * w], 0.0).T.astype(BF16)


def _qkvproj(x3d, g, w_qkv, layer):
    bsz, lp, d = x3d.shape
    w = w_qkv.shape[2] // 3
    tm = _pick_tile(lp, (768, 512, 256))
    q_scale = math.log2(math.e) / math.sqrt(w // SB_HEADS)
    row = pl.BlockSpec((None, tm, w), lambda b, t: (b, t, 0))
    colm = pl.BlockSpec((None, w, tm), lambda b, t: (b, 0, t))
    return pl.pallas_call(
        functools.partial(_qkvproj_kernel, q_scale=q_scale),
        grid=(bsz, lp // tm),
        in_specs=[pl.BlockSpec((None, tm, d), lambda b, t: (b, t, 0)),
                  pl.BlockSpec((1, d), lambda b, t: (0, 0)),
                  pl.BlockSpec((None,) + w_qkv.shape[1:], lambda b, t: (layer, 0, 0))],
        out_specs=[colm, row, colm],
        out_shape=[jax.ShapeDtypeStruct((bsz, w, lp), BF16),
                   jax.ShapeDtypeStruct((bsz, lp, w), BF16),
                   jax.ShapeDtypeStruct((bsz, w, lp), BF16)],
        compiler_params=_params("parallel", "parallel"),
        name="qkvproj",
    )(x3d, g, w_qkv)


HG_LEVELS = (64, 32, 16, 8, 4, 2, 1)
LOG2E = math.log2(math.e)


def _hgrn2_kernel(q_ref, f_ref, i_ref, g_ref, lb_ref, ng_ref, o_ref, st_ref):
    step_i = pl.program_id(1)
    tm, width = q_ref.shape
    dk = BLOCK
    tiles = BLOCK // SUB

    @pl.when(step_i == 0)
    def _():
        st_ref[...] = jnp.zeros_like(st_ref)

    r2 = lax.broadcasted_iota(jnp.int32, (BLOCK, BLOCK), 0)
    c2 = lax.broadcasted_iota(jnp.int32, (BLOCK, BLOCK), 1)
    tril = (c2 <= r2).astype(BF16)
    differ = r2 ^ c2
    level = jnp.zeros((BLOCK, BLOCK), jnp.int32)
    for c in reversed(HG_LEVELS):
        level = jnp.where(differ >= c, c, level)
    level = jnp.where(c2 > r2, -1, level)
    sub_row = lax.broadcasted_iota(jnp.int32, (1, SUB, 1), 1)
    sign_bit = jnp.uint32(0x80000000)
    lb = lb_ref[...]

    def chunk(ci, carry):
        r0 = pl.multiple_of(ci * BLOCK, BLOCK)
        rows = pl.ds(r0, BLOCK)
        pos = lax.broadcasted_iota(jnp.int32, (BLOCK, 1), 0) + (step_i * tm + r0)
        valid = pos >= N_MASKED
        xf = f_ref[rows, :].astype(F32)
        log_f = jnp.where(valid, jnp.log(lb + (1.0 - lb) * jax.nn.sigmoid(xf)), 0.0)
        kk = jnp.where(valid, (1.0 - lb) * jax.nn.sigmoid(-xf), 0.0)
        vv = jnp.where(valid, i_ref[rows, :].astype(F32), 0.0)
        qq = q_ref[rows, :].astype(F32)
        b2 = _dot_split(tril, log_f) * LOG2E

        issued = {}

        def issue(h):
            sl = slice(h * dk, (h + 1) * dk)
            q, k, v, b = qq[:, sl], kk[:, sl], vv[:, sl], b2[:, sl]
            b3 = b.reshape(tiles, SUB, dk)

            def tile_row(r):
                return jnp.broadcast_to(b3[:, r:r + 1, :], b3.shape)

            last3 = tile_row(SUB - 1)
            scores = {0: _dot_nt(q.astype(BF16), k.astype(BF16))}
            for c in HG_LEVELS:
                if c >= SUB:
                    t8 = c // SUB
                    idx = [(t // t8) * t8 - 1 if (t // t8) % 2 else (t // t8 + 1) * t8 - 1
                           for t in range(tiles)]
                    ref3 = jnp.concatenate([last3[i:i + 1] for i in idx], axis=0)
                elif c == 4:
                    ref3 = tile_row(3)
                elif c == 2:
                    ref3 = jnp.where(sub_row < 4, tile_row(1), tile_row(5))
                else:
                    ref3 = jnp.where(sub_row % 2 == 1, pltpu.roll(b3, 1, axis=1), b3)
                d = b - ref3.reshape(BLOCK, dk)
                g = jnp.exp2(lax.bitcast_convert_type(lax.bitcast_convert_type(d, jnp.uint32) | sign_bit, F32))
                scores[c] = _dot_nt((q * g).astype(BF16), (k * g).astype(BF16))

            b_last = b[BLOCK - 1:BLOCK, :]
            st = st_ref[h]
            o_inter = _dot_nt((q * jnp.exp2(b)).astype(BF16), st.astype(BF16))
            k_out = (k * jnp.exp2(b_last - b)).astype(BF16)
            st_ref[h] = st * jnp.exp2(b_last) + _dot(v.T.astype(BF16), k_out)
            issued[h] = (scores, o_inter, v.astype(BF16))

        def finish(h):
            sl = slice(h * dk, (h + 1) * dk)
            scores, o_inter, v16 = issued.pop(h)
            p = jnp.zeros((BLOCK, BLOCK), F32)
            for c in (0,) + HG_LEVELS:
                p = jnp.where(level == c, scores[c], p)
            o = o_inter + _dot(p.astype(BF16), v16)
            o = o * lax.rsqrt(jnp.mean(o * o, axis=-1, keepdims=True) + EPS) * ng_ref[:, sl]
            o_ref[rows, sl] = (o * jax.nn.sigmoid(g_ref[rows, sl].astype(F32))).astype(o_ref.dtype)

        for h in range(HG_HEADS + 1):
            if h < HG_HEADS:
                issue(h)
            if h >= 1:
                finish(h - 1)
        return carry

    lax.fori_loop(0, tm // BLOCK, chunk, 0)


def _hgrn2(proj3, lb, ng):
    bsz, lp, _ = proj3.shape
    w = lb.shape[-1]
    tm = _pick_tile(lp, (768, 512, 256))

    def col(cb):
        return pl.BlockSpec((None, tm, w), lambda b, c: (b, c, cb))

    vec = pl.BlockSpec((1, w), lambda b, c: (0, 0))
    return pl.pallas_call(
        _hgrn2_kernel,
        grid=(bsz, lp // tm),
        in_specs=[col(_C_HG_Q), col(_C_HG_F), col(_C_HG_I), col(_C_HG_G), vec, vec],
        out_specs=pl.BlockSpec((None, tm, w), lambda b, c: (b, c, 0)),
        out_shape=jax.ShapeDtypeStruct((bsz, lp, w), BF16),
        scratch_shapes=[pltpu.VMEM((HG_HEADS, BLOCK, BLOCK), F32)],
        compiler_params=_params("parallel", "arbitrary"),
        name="hgrn2",
    )(proj3, proj3, proj3, proj3, lb, ng)


POOL_HALO = 16
CONV_HALO = 8


def _local_kernel(pv_ref, ch_ref, cb_ref, cc_ref, pw_ref, ps_ref, cw_ref, po_ref, co_ref,
                  vbuf, ubuf):
    t = pl.program_id(1)
    tm = pv_ref.shape[0]
    grp = pv_ref.shape[1] // len(POOL_WINDOWS)

    @pl.when(t == 0)
    def _():
        vbuf[0:POOL_HALO, :] = jnp.zeros((POOL_HALO, vbuf.shape[1]), F32)
        ubuf[0:CONV_HALO, :] = jnp.zeros((CONV_HALO, ubuf.shape[1]), F32)

    pos = lax.broadcasted_iota(jnp.int32, (tm, 1), 0) + t * tm
    valid = pos >= N_MASKED
    vf = jnp.where(valid, pv_ref[...].astype(F32), 0.0)
    vbuf[POOL_HALO:POOL_HALO + tm, :] = vf
    cnt = jnp.maximum(pos - (N_MASKED - 1), 0).astype(F32)

    for gi, w in enumerate(POOL_WINDOWS):
        sl = slice(gi * grp, (gi + 1) * grp)
        acc = vf[:, sl]
        for d in range(1, w):
            acc = acc + vbuf[POOL_HALO - d:POOL_HALO - d + tm, sl]
        cnt_prev = jnp.maximum(pos - w - (N_MASKED - 1), 0).astype(F32)
        n_win = jnp.maximum(cnt - cnt_prev, 1.0)
        u = acc / n_win - vf[:, sl]
        y = _dot(u.astype(BF16), pw_ref[gi])
        po_ref[:, sl] = (y * ps_ref[:, sl]).astype(po_ref.dtype)

    u = jnp.where(valid, cc_ref[...].astype(F32) * ch_ref[...].astype(F32), 0.0)
    ubuf[CONV_HALO:CONV_HALO + tm, :] = u
    y = u * cw_ref[CONV_WIDTH - 1:CONV_WIDTH, :]
    for d in range(1, CONV_WIDTH):
        y = y + ubuf[CONV_HALO - d:CONV_HALO - d + tm, :] * cw_ref[CONV_WIDTH - 1 - d:CONV_WIDTH - d, :]
    co_ref[...] = (cb_ref[...].astype(F32) * y).astype(co_ref.dtype)

    vbuf[0:POOL_HALO, :] = vbuf[tm:tm + POOL_HALO, :]
    ubuf[0:CONV_HALO, :] = ubuf[tm:tm + CONV_HALO, :]


def _local(proj3, pool_w, pool_scale, conv_w):
    bsz, lp, _ = proj3.shape
    w = pool_scale.shape[-1]
    tm = _pick_tile(lp, (768, 512, 256))

    def col(cb):
        return pl.BlockSpec((None, tm, w), lambda b, t: (b, t, cb))

    out = pl.BlockSpec((None, tm, w), lambda b, t: (b, t, 0))
    shp = jax.ShapeDtypeStruct((bsz, lp, w), BF16)
    return pl.pallas_call(
        _local_kernel,
        grid=(bsz, lp // tm),
        in_specs=[col(_C_POOL_V), col(_C_SC_H), col(_C_SC_B), col(_C_SC_C),
                  pl.BlockSpec(pool_w.shape, lambda b, t: (0, 0, 0)),
                  pl.BlockSpec((1, w), lambda b, t: (0, 0)),
                  pl.BlockSpec(conv_w.shape, lambda b, t: (0, 0))],
        out_specs=[out, out],
        out_shape=[shp, shp],
        scratch_shapes=[pltpu.VMEM((tm + POOL_HALO, w), F32), pltpu.VMEM((tm + CONV_HALO, w), F32)],
        compiler_params=_params("parallel", "arbitrary"),
        name="local",
    )(proj3, proj3, proj3, proj3, pool_w, pool_scale, conv_w)


def _attn_kernel(qt_ref, k_ref, vt_ref, o_ref, qm_ref, acc_ref, carry_ref, lb_ref, sp_ref):
    qi = pl.program_id(1)
    blk = ATT_BLK
    width = qt_ref.shape[0]
    n_pair = width // LANES
    dh = width // SB_HEADS

    acc_ref[...] = jnp.zeros_like(acc_ref)
    carry_ref[...] = jnp.zeros_like(carry_ref)
    zeros = jnp.zeros((dh, blk), BF16)
    for p in range(n_pair):
        qm_ref[2 * p] = jnp.concatenate([qt_ref[p * LANES:p * LANES + dh, :], zeros], axis=0)
        qm_ref[2 * p + 1] = jnp.concatenate([zeros, qt_ref[p * LANES + dh:(p + 1) * LANES, :]], axis=0)

    rr = lax.broadcasted_iota(jnp.int32, (blk, blk), 0)
    cc = lax.broadcasted_iota(jnp.int32, (blk, blk), 1)
    later = (cc > rr).astype(BF16)
    sign_bit = jnp.uint32(0x80000000)

    def step(score_blk, score_par, out_blk, out_par, diagonal=False):
        later_sums, scores = {}, {}

        def issue_later_sums(h):
            later_sums[h] = _dot(later, sp_ref[out_par, h])

        def issue_scores(h):
            ks = pl.multiple_of(score_blk * blk, blk)
            p = h // 2
            scores[h] = _dot(k_ref[pl.ds(ks, blk), p * LANES:(p + 1) * LANES], qm_ref[h])

        def finish_scores(h):
            z = scores[h]
            if diagonal:
                key_loc = lax.broadcasted_iota(jnp.int32, (blk, 1), 0)
                q_loc = lax.broadcasted_iota(jnp.int32, (1, blk), 1)
                z = jnp.where(key_loc < q_loc, z, MASKED_SCORE)
            neg_abs = lax.bitcast_convert_type(
                lax.bitcast_convert_type(z, jnp.uint32) | sign_bit, F32)
            sp = jnp.maximum(z, 0.0) + jnp.log2(1.0 + jnp.exp2(neg_abs))
            lb_ref[score_par, h] = z - sp
            sp_ref[score_par, h] = sp.astype(BF16)

        def finish_output(h):
            ks = pl.multiple_of(out_blk * blk, blk)
            p, hh = h // 2, h % 2
            later_sum = later_sums[h]
            a = jnp.exp2(lb_ref[out_par, h] - later_sum).astype(BF16)
            carry = carry_ref[h]
            first_sp = sp_ref[out_par, h, 0:1, :].astype(F32)
            carry_ref[h] = carry + (later_sum[0:1, :] + first_sp)
            vt = vt_ref[p * LANES:(p + 1) * LANES, pl.ds(ks, blk)]
            ot = _dot(vt, a)
            scale = jnp.tile(jnp.exp2(-carry), (dh // 8, 1))
            acc_ref[p, hh * dh:(hh + 1) * dh, :] += ot[hh * dh:(hh + 1) * dh, :] * scale

        for t in range(SB_HEADS + ATT_LOOKAHEAD):
            if t < SB_HEADS:
                if out_blk is not None:
                    issue_later_sums(t)
                if score_blk is not None:
                    issue_scores(t)
            h = t - ATT_LOOKAHEAD
            if h >= 0:
                if out_blk is not None:
                    finish_output(h)
                if score_blk is not None:
                    finish_scores(h)

    step(qi, 0, None, None, diagonal=True)

    @pl.when(qi >= 1)
    def _():
        step(qi - 1, 1, qi, 0)

    def keep_walking(state):
        c, min_carry = state
        return jnp.logical_and(c >= 0, min_carry < ZERO_WEIGHT_CARRY)

    def walk(state):
        c, _ = state
        buf = (qi - c) & 1
        for par in range(2):
            @pl.when(buf == par)
            def _():
                step(None, None, c, par)
        min_carry = jnp.min(carry_ref[...])
        more = jnp.logical_and(c >= 1, min_carry < ZERO_WEIGHT_CARRY)
        for par in range(2):
            @pl.when(jnp.logical_and(more, buf == par))
            def _():
                step(c - 1, 1 - par, None, None)
        return c - 1, min_carry

    lax.while_loop(keep_walking, walk, (jnp.maximum(qi - 1, 0), jnp.float32(0.0)))

    for p in range(n_pair):
        o_ref[:, p * LANES:(p + 1) * LANES] = acc_ref[p].T.astype(o_ref.dtype)


def _attn(qt, k, vt):
    bsz, w, lp = qt.shape
    return pl.pallas_call(
        _attn_kernel,
        grid=(bsz, lp // ATT_BLK),
        in_specs=[pl.BlockSpec((None, w, ATT_BLK), lambda b, i: (b, 0, i)),
                  pl.BlockSpec((None, lp, w), lambda b, i: (b, 0, 0)),
                  pl.BlockSpec((None, w, lp), lambda b, i: (b, 0, 0))],
        out_specs=pl.BlockSpec((None, ATT_BLK, w), lambda b, i: (b, i, 0)),
        out_shape=jax.ShapeDtypeStruct((bsz, lp, w), BF16),
        scratch_shapes=[pltpu.VMEM((SB_HEADS, LANES, ATT_BLK), BF16),
                        pltpu.VMEM((w // LANES, LANES, ATT_BLK), F32),
                        pltpu.VMEM((SB_HEADS, 8, ATT_BLK), F32),
                        pltpu.VMEM((2, SB_HEADS, ATT_BLK, ATT_BLK), F32),
                        pltpu.VMEM((2, SB_HEADS, ATT_BLK, ATT_BLK), BF16)],
        compiler_params=_params("parallel", "arbitrary"),
        name="attn",
    )(qt, k, vt)


FF_CHUNK = 1024


def _mixmlp_kernel(x_ref, gate_ref, b0_ref, b1_ref, b2_ref, b3_ref, wb_ref, wo_ref, g2_ref, wu_ref, wd_ref,
                   fg_ref, o_ref, *, final_norm):
    d = x_ref.shape[1]
    mixed = jnp.zeros(x_ref.shape, F32)
    for n, b_ref in enumerate((b0_ref, b1_ref, b2_ref, b3_ref)):
        y = _dot(b_ref[...], wb_ref[n])
        mixed = mixed + jax.nn.sigmoid(gate_ref[:, n * d:(n + 1) * d].astype(F32)) * y
    x = x_ref[...] + _dot(mixed.astype(BF16), wo_ref[...])

    h = _rms(x, g2_ref[...]).astype(BF16)
    acc = x
    for c in range(wu_ref.shape[1] // FF_CHUNK):
        a = jnp.maximum(_dot(h, wu_ref[:, c * FF_CHUNK:(c + 1) * FF_CHUNK]), 0.0)
        acc = acc + _dot((a * a).astype(BF16), wd_ref[c * FF_CHUNK:(c + 1) * FF_CHUNK, :])
    if final_norm:
        acc = _rms(acc, fg_ref[...])
    o_ref[...] = acc


def _mixmlp(x2d, proj2d, branches, w_branch, w_o, g2, w_up, w_down, final_g, final_norm, layer):
    n, d = x2d.shape
    w = branches[0].shape[-1]
    tm = _pick_tile(n, (512, 256, 128))
    row = lambda i: (i, 0)
    vec = pl.BlockSpec((1, d), lambda i: (0, 0))
    const1 = pl.Buffered(1)

    def layer_weight(a):
        zeros = (0,) * (a.ndim - 1)
        return pl.BlockSpec((None,) + a.shape[1:], lambda i: (layer,) + zeros, pipeline_mode=const1)

    return pl.pallas_call(
        functools.partial(_mixmlp_kernel, final_norm=final_norm),
        grid=(n // tm,),
        in_specs=[pl.BlockSpec((tm, d), row),
                  pl.BlockSpec((tm, N_BRANCH * d), row)] +
                 [pl.BlockSpec((tm, w), row)] * N_BRANCH +
                 [layer_weight(w_branch), layer_weight(w_o), vec, layer_weight(w_up), layer_weight(w_down),
                  vec],
        out_specs=pl.BlockSpec((tm, d), row),
        out_shape=jax.ShapeDtypeStruct((n, d), F32),
        compiler_params=_params("parallel"),
        name="mixmlp",
    )(x2d, proj2d, *branches, w_branch, w_o, g2, w_up, w_down, final_g)


def kernel(x, meta_tokens, lb_logits, norm1_g, w_in, hg_norm_g, pool_w, pool_scale, conv_w,
           w_branch, w_o, norm2_g, w_up, w_down, final_norm_g):
    bsz, seq, d = x.shape
    depth = w_in.shape[0]
    w = d // 2
    assert w % LANES == 0 and w // HG_HEADS == BLOCK and w // len(POOL_WINDOWS) == LANES
    assert w_in.shape[2] == N_SPLIT * w + N_BRANCH * d

    length = FRONT + seq
    lp = -(-length // ATT_BLK) * ATT_BLK
    h = lax.dynamic_update_slice(jnp.zeros((bsz, lp, d), x.dtype), x, (0, FRONT, 0))
    h = lax.dynamic_update_slice(
        h, jnp.broadcast_to(meta_tokens.astype(x.dtype)[None], (bsz, N_META, d)), (0, N_MASKED, 0))
    h = h.reshape(bsz * lp, d)

    cum = jnp.cumsum(jax.nn.softmax(lb_logits.astype(F32), axis=0), axis=0)
    lower_bounds = cum - cum[0]

    w_in16 = w_in.astype(BF16)
    w_main = jnp.concatenate([w_in16[:, :, N_SPLIT * w:], w_in16[:, :, :5 * w], w_in16[:, :, 8 * w:N_SPLIT * w]],
                             axis=2)
    w_qkv = w_in16[:, :, 5 * w:8 * w]
    w_branch16, w_o16 = w_branch.astype(BF16), w_o.astype(BF16)
    w_up16, w_down16 = w_up.astype(BF16), w_down.astype(BF16)
    pool_w16 = pool_w.astype(BF16)
    fg = final_norm_g.reshape(1, d)

    for layer in range(depth):
        g1 = norm1_g[layer].reshape(1, d)
        proj = _inproj(h, g1, w_main, layer)
        qt, k, vt = _qkvproj(h.reshape(bsz, lp, d), g1, w_qkv, layer)
        proj3 = proj.reshape(bsz, lp, -1)
        b_hg = _hgrn2(proj3, lower_bounds[layer].reshape(1, w), hg_norm_g[layer].reshape(1, w))
        b_pool, b_conv = _local(proj3, pool_w16[layer], pool_scale[layer].reshape(1, w), conv_w[layer])
        b_sb = _attn(qt, k, vt)
        branches = [b.reshape(bsz * lp, w) for b in (b_hg, b_pool, b_sb, b_conv)]
        h = _mixmlp(h, proj, branches, w_branch16, w_o16, norm2_g[layer].reshape(1, d),
                    w_up16, w_down16, fg, final_norm=(layer == depth - 1), layer=layer)

    return h.reshape(bsz, lp, d)[:, FRONT:length]
```
